```python
import jax
import jax.numpy as jnp
from jax import lax
import numpy as np

D_MODEL = 1024
BATCH = 16
SEQ = 2048
DEPTH = 2

N_EVEN = (DEPTH + 1) // 2
N_ODD = DEPTH // 2
EPS = 1e-6
NEG_INF = -1e30

SG_HEADS = 4
SG_HEAD_DIM = D_MODEL // 8
SG_WIDTH = SG_HEADS * SG_HEAD_DIM
SG_CHUNK = 128
SC_HEADS = 4
SC_HEAD_DIM = D_MODEL // 8
SC_WIDTH = SC_HEADS * SC_HEAD_DIM
CONV_WIDTH = 3
EVEN_IN = 2 * SG_WIDTH + 3 * SC_WIDTH
EVEN_MIX = SG_WIDTH + SC_WIDTH

POOL_WINDOWS = (2, 4, 8, 16)
POOL_GROUPS = len(POOL_WINDOWS)
POOL_GROUP_DIM = D_MODEL // 16
POOL_WIDTH = POOL_GROUPS * POOL_GROUP_DIM
MLA_HEADS = 6
Q_LORA = 3 * D_MODEL // 8
KV_LORA = D_MODEL // 4
QK_NOPE = 128
QK_ROPE = 64
QK_DIM = QK_NOPE + QK_ROPE
V_DIM = 128
ROPE_THETA = 10000.0
Q_BLOCK = 128
ODD_IN = POOL_WIDTH + Q_LORA + KV_LORA + QK_ROPE
ODD_MIX = POOL_WIDTH + MLA_HEADS * V_DIM

D_FF = ((8 * D_MODEL + 3 * 256 - 1) // (3 * 256)) * 256

kernel_name = 'hybrid_sgu_conv_pool_mla_trunk'


def rms_norm(x, g):
    xf = x.astype(jnp.float32)
    y = xf * lax.rsqrt(jnp.mean(xf * xf, axis=-1, keepdims=True) + EPS)
    return (y * g.astype(jnp.float32)).astype(x.dtype)


def layer_norm(x, g):
    xf = x.astype(jnp.float32)
    mu = jnp.mean(xf, axis=-1, keepdims=True)
    xc = xf - mu
    y = xc * lax.rsqrt(jnp.mean(xc * xc, axis=-1, keepdims=True) + EPS)
    return (y * g.astype(jnp.float32)).astype(x.dtype)


def spatial_gating(u, v, ln_g, w_s, b_s):
    bsz, s, _ = v.shape
    n_chunks = s // SG_CHUNK
    v = layer_norm(v.reshape(bsz, s, SG_HEADS, SG_HEAD_DIM), ln_g.reshape(SG_HEADS, SG_HEAD_DIM))
    v = v.reshape(bsz, n_chunks, SG_CHUNK, SG_HEADS, SG_HEAD_DIM)
    causal = jnp.tril(jnp.ones((SG_CHUNK, SG_CHUNK), dtype=bool))
    w = jnp.where(causal[None], w_s, 0.0).astype(v.dtype)
    mixed = jnp.einsum('hts,bnshd->bnthd', w, v) + b_s.T.astype(v.dtype)[None, None, :, :, None]
    return u * mixed.reshape(bsz, s, SG_WIDTH)


def short_conv(b_gate, c_gate, h, conv_w):
    z = c_gate * h
    y = lax.conv_general_dilated(
        z, conv_w[:, None, :].astype(z.dtype), window_strides=(1,),
        padding=[(CONV_WIDTH - 1, 0)], dimension_numbers=('NWC', 'WIO', 'NWC'),
        feature_group_count=SC_WIDTH)
    return b_gate * y


def multiscale_pool(z, lin_w, scale):
    bsz, s, _ = z.shape
    zf = z.astype(jnp.float32)
    cs = jnp.pad(jnp.cumsum(zf, axis=1), ((0, 0), (1, 0), (0, 0)))
    t = jnp.arange(1, s + 1, dtype=jnp.float32)[None, :, None]
    groups = []
    for g, w in enumerate(POOL_WINDOWS):
        lo, hi = g * POOL_GROUP_DIM, (g + 1) * POOL_GROUP_DIM
        c = cs[..., lo:hi]
        lower = jnp.pad(c[:, :s + 1 - w], ((0, 0), (w - 1, 0), (0, 0)))
        mean = (c[:, 1:] - lower) / jnp.minimum(t, float(w))
        groups.append(mean - zf[..., lo:hi])
    pooled = jnp.stack(groups, axis=2).astype(z.dtype)
    out = jnp.einsum('bsgi,gio->bsgo', pooled, lin_w)
    return out.reshape(bsz, s, POOL_WIDTH) * scale


def rope_tables(positions):
    inv_freq = ROPE_THETA ** (-jnp.arange(0, QK_ROPE, 2, dtype=jnp.float32) / QK_ROPE)
    ang = positions.astype(jnp.float32)[..., None] * inv_freq
    return jnp.cos(ang), jnp.sin(ang)


def apply_rope(x, cos, sin):
    c = cos[:, :, None, :].astype(x.dtype)
    s = sin[:, :, None, :].astype(x.dtype)
    x1, x2 = jnp.split(x, 2, axis=-1)
    return jnp.concatenate([x1 * c - x2 * s, x2 * c + x1 * s], axis=-1)


def latent_attention(q_lat, kv_lat, k_rope, cos, sin, q_a_g, q_b, kv_a_g, kv_b, q_g, k_g):
    bsz, s, _ = q_lat.shape
    q = (rms_norm(q_lat, q_a_g) @ q_b).reshape(bsz, s, MLA_HEADS, QK_DIM)
    kv = (rms_norm(kv_lat, kv_a_g) @ kv_b).reshape(bsz, s, MLA_HEADS, QK_NOPE + V_DIM)
    k_nope, v = kv[..., :QK_NOPE], kv[..., QK_NOPE:]
    k = jnp.concatenate(
        [k_nope, jnp.broadcast_to(k_rope[:, :, None, :], (bsz, s, MLA_HEADS, QK_ROPE))], axis=-1)
    q = rms_norm(q, q_g)
    k = rms_norm(k, k_g)
    q = jnp.concatenate([q[..., :QK_NOPE], apply_rope(q[..., QK_NOPE:], cos, sin)], axis=-1)
    k = jnp.concatenate([k[..., :QK_NOPE], apply_rope(k[..., QK_NOPE:], cos, sin)], axis=-1)
    scale = QK_DIM ** -0.5
    outs = []
    for i in range(s // Q_BLOCK):
        q0, k_end = i * Q_BLOCK, (i + 1) * Q_BLOCK
        logits = jnp.einsum('bqhd,bkhd->bhqk', q[:, q0:k_end], k[:, :k_end]).astype(jnp.float32) * scale
        mask = (q0 + jnp.arange(Q_BLOCK))[:, None] >= jnp.arange(k_end)[None, :]
        p = jax.nn.softmax(jnp.where(mask, logits, NEG_INF), axis=-1).astype(v.dtype)
        outs.append(jnp.einsum('bhqk,bkhd->bqhd', p, v[:, :k_end]))
    return jnp.concatenate(outs, axis=1).reshape(bsz, s, MLA_HEADS * V_DIM)


def swiglu(h, w_gate, w_up, w_down):
    return (jax.nn.silu(h @ w_gate) * (h @ w_up)) @ w_down


def setup_inputs(seed: int = 0) -> dict:
    key = jax.random.key(seed)
    k = jax.random.split(key, 23)
    f32 = jnp.float32

    def nrm(kk, shape, fan_in):
        return jax.random.normal(kk, shape, f32) * (fan_in ** -0.5)

    def gain(kk, shape, noise=0.02):
        return 1.0 + noise * jax.random.normal(kk, shape, f32)

    x = jax.random.normal(k[0], (BATCH, SEQ, D_MODEL), f32)
    positions = (jnp.arange(SEQ, dtype=jnp.int32)[None, :]
                 + jax.random.randint(k[1], (BATCH, 1), 0, SEQ, dtype=jnp.int32))
    return {
        'x': x,
        'positions': positions,
        'mix_norm': gain(k[2], (DEPTH, D_MODEL)),
        'ffn_norm': gain(k[3], (DEPTH, D_MODEL)),
        'even_w_in': nrm(k[4], (N_EVEN, D_MODEL, EVEN_IN), D_MODEL),
        'sg_ln_g': gain(k[5], (N_EVEN, SG_WIDTH)),
        'sg_w_s': nrm(k[6], (N_EVEN, SG_HEADS, SG_CHUNK, SG_CHUNK), SG_CHUNK),
        'sg_b_s': gain(k[7], (N_EVEN, SG_HEADS, SG_CHUNK), 0.1),
        'sc_conv_w': nrm(k[8], (N_EVEN, CONV_WIDTH, SC_WIDTH), CONV_WIDTH),
        'even_w_out': nrm(k[9], (N_EVEN, EVEN_MIX, D_MODEL), EVEN_MIX),
        'odd_w_in': nrm(k[10], (N_ODD, D_MODEL, ODD_IN), D_MODEL),
        'pool_w': nrm(k[11], (N_ODD, POOL_GROUPS, POOL_GROUP_DIM, POOL_GROUP_DIM), POOL_GROUP_DIM),
        'pool_scale': gain(k[12], (N_ODD, POOL_WIDTH), 0.1),
        'q_a_norm': gain(k[13], (N_ODD, Q_LORA)),
        'q_b': nrm(k[14], (N_ODD, Q_LORA, MLA_HEADS * QK_DIM), Q_LORA),
        'kv_a_norm': gain(k[15], (N_ODD, KV_LORA)),
        'kv_b': nrm(k[16], (N_ODD, KV_LORA, MLA_HEADS * (QK_NOPE + V_DIM)), KV_LORA),
        'q_norm': gain(k[17], (N_ODD, QK_DIM)),
        'k_norm': gain(k[18], (N_ODD, QK_DIM)),
        'odd_w_out': nrm(k[19], (N_ODD, ODD_MIX, D_MODEL), ODD_MIX),
        'ffn_w_gate': nrm(k[20], (DEPTH, D_MODEL, D_FF), D_MODEL),
        'ffn_w_up': nrm(k[21], (DEPTH, D_MODEL, D_FF), D_MODEL),
        'ffn_w_down': nrm(k[22], (DEPTH, D_FF, D_MODEL), D_FF),
    }


def reference(x, positions, mix_norm, ffn_norm, even_w_in, sg_ln_g, sg_w_s, sg_b_s, sc_conv_w,
              even_w_out, odd_w_in, pool_w, pool_scale, q_a_norm, q_b, kv_a_norm, kv_b, q_norm,
              k_norm, odd_w_out, ffn_w_gate, ffn_w_up, ffn_w_down):
    cos, sin = rope_tables(positions)
    for layer in range(DEPTH):
        i = layer // 2
        h = rms_norm(x, mix_norm[layer])
        if layer % 2 == 0:
            proj = h @ even_w_in[i]
            u, v, b_gate, c_gate, hv = jnp.split(
                proj, [SG_WIDTH, 2 * SG_WIDTH, 2 * SG_WIDTH + SC_WIDTH, 2 * SG_WIDTH + 2 * SC_WIDTH], axis=-1)
            a_out = spatial_gating(jax.nn.gelu(u, approximate=False), jax.nn.gelu(v, approximate=False),
                                   sg_ln_g[i], sg_w_s[i], sg_b_s[i])
            b_out = short_conv(b_gate, c_gate, hv, sc_conv_w[i])
            x = x + jnp.concatenate([a_out, b_out], axis=-1) @ even_w_out[i]
        else:
            proj = h @ odd_w_in[i]
            z_pool, q_lat, kv_lat, k_rope = jnp.split(
                proj, [POOL_WIDTH, POOL_WIDTH + Q_LORA, POOL_WIDTH + Q_LORA + KV_LORA], axis=-1)
            c_out = multiscale_pool(z_pool, pool_w[i], pool_scale[i])
            d_out = latent_attention(q_lat, kv_lat, k_rope, cos, sin, q_a_norm[i], q_b[i],
                                     kv_a_norm[i], kv_b[i], q_norm[i], k_norm[i])
            x = x + jnp.concatenate([c_out, d_out], axis=-1) @ odd_w_out[i]
        h = rms_norm(x, ffn_norm[layer])
        x = x + swiglu(h, ffn_w_gate[layer], ffn_w_up[layer], ffn_w_down[layer])
    return x
```

```python
import functools
import math

import jax
import jax.numpy as jnp
from jax import lax
from jax.experimental import pallas as pl
from jax.experimental.pallas import tpu as pltpu

F32 = jnp.float32
BF16 = jnp.bfloat16

EPS = 1e-6
MASK_VALUE = -1e30
LANES = 128
SG_CHUNK = 128
SG_HEADS = 4
CONV_WIDTH = 3
POOL_WINDOWS = (2, 4, 8, 16)
POOL_GROUP_DIM = 64
MLA_HEADS = 6
QK_NOPE = 128
QK_ROPE = 64
QK_DIM = QK_NOPE + QK_ROPE
V_DIM = 128
ROPE_THETA = 10000.0
CARRY_ROWS = 8
POOL_CARRY_ROWS = 16
VMEM_LIMIT_BYTES = 56 * 1024 * 1024


def _rms_norm(x, g):
    ms = jnp.mean(x * x, axis=-1, keepdims=True)
    return x * lax.rsqrt(ms + EPS) * g


def _gelu(x):
    return 0.5 * x * (1.0 + lax.erf(x * math.sqrt(0.5)))


def _dot(a, b):
    return jnp.dot(a, b, preferred_element_type=F32)


def _full_spec(arr):
    nd = arr.ndim
    return pl.BlockSpec(arr.shape, lambda *_: (0,) * nd)


def _params():
    return pltpu.CompilerParams(
        dimension_semantics=("arbitrary", "arbitrary"),
        vmem_limit_bytes=VMEM_LIMIT_BYTES)


def _even_mixer_kernel(x_ref, g_ref, w_in_ref, ln_g_ref, w_s_ref, b_st_ref,
                       conv_w_ref, mix_ref, zbuf_ref, *, tm):
    sg_w = SG_HEADS * LANES
    h = _rms_norm(x_ref[...], g_ref[...]).astype(BF16)
    proj = _dot(h, w_in_ref[...])
    u = _gelu(proj[:, 0:sg_w])
    v = _gelu(proj[:, sg_w:2 * sg_w])
    sc_w = (proj.shape[1] - 2 * sg_w) // 3
    b_gate = proj[:, 2 * sg_w:2 * sg_w + sc_w]
    c_gate = proj[:, 2 * sg_w + sc_w:2 * sg_w + 2 * sc_w]
    hv = proj[:, 2 * sg_w + 2 * sc_w:]

    row = lax.broadcasted_iota(jnp.int32, (SG_CHUNK, SG_CHUNK), 0)
    col = lax.broadcasted_iota(jnp.int32, (SG_CHUNK, SG_CHUNK), 1)
    causal = row >= col
    for hd in range(SG_HEADS):
        cs = slice(hd * LANES, (hd + 1) * LANES)
        vh = v[:, cs]
        mu = jnp.mean(vh, axis=-1, keepdims=True)
        xc = vh - mu
        var = jnp.mean(xc * xc, axis=-1, keepdims=True)
        vn = (xc * lax.rsqrt(var + EPS) * ln_g_ref[:, cs]).astype(BF16)
        w = jnp.where(causal, w_s_ref[hd], 0.0).astype(BF16)
        bias = b_st_ref[:, hd:hd + 1]
        for c in range(tm // SG_CHUNK):
            rs = slice(c * SG_CHUNK, (c + 1) * SG_CHUNK)
            mixed = _dot(w, vn[rs]) + bias
            mix_ref[rs, cs] = (u[rs, cs] * mixed).astype(BF16)

    @pl.when(pl.program_id(1) == 0)
    def _():
        zbuf_ref[0:CARRY_ROWS, :] = jnp.zeros((CARRY_ROWS, sc_w), F32)

    z = c_gate * hv
    zbuf_ref[CARRY_ROWS:CARRY_ROWS + tm, :] = z
    y = conv_w_ref[CONV_WIDTH - 1:CONV_WIDTH, :] * z
    for k in range(CONV_WIDTH - 1):
        shift = CONV_WIDTH - 1 - k
        y = y + conv_w_ref[k:k + 1, :] * zbuf_ref[CARRY_ROWS - shift:CARRY_ROWS - shift + tm, :]
    mix_ref[:, sg_w:] = (b_gate * y).astype(BF16)
    zbuf_ref[0:CARRY_ROWS, :] = zbuf_ref[tm:tm + CARRY_ROWS, :]


def _even_mixer(x, g, w_in, ln_g, w_s, b_st, conv_w, *, tm):
    b, s, d = x.shape
    sc_w = conv_w.shape[1]
    mix_w = SG_HEADS * LANES + sc_w
    tok = pl.BlockSpec((None, tm, d), lambda i, j: (i, j, 0))
    return pl.pallas_call(
        functools.partial(_even_mixer_kernel, tm=tm),
        grid=(b, s // tm),
        in_specs=[tok, _full_spec(g), _full_spec(w_in), _full_spec(ln_g),
                  _full_spec(w_s), _full_spec(b_st), _full_spec(conv_w)],
        out_specs=pl.BlockSpec((None, tm, mix_w), lambda i, j: (i, j, 0)),
        out_shape=jax.ShapeDtypeStruct((b, s, mix_w), BF16),
        scratch_shapes=[pltpu.VMEM((tm + CARRY_ROWS, sc_w), F32)],
        compiler_params=_params(),
        name="even_mixer",
    )(x, g, w_in, ln_g, w_s, b_st, conv_w)


def _post_kernel(*refs, n_mix, ff_chunk):
    x_ref = refs[0]
    mix_refs = refs[1:1 + n_mix]
    wo_ref, g_ref, wg_ref, wu_ref, wd_ref, o_ref = refs[1 + n_mix:]
    mix = [m_ref[...] for m_ref in mix_refs]
    mix = mix[0] if n_mix == 1 else jnp.concatenate(mix, axis=1)
    x1 = x_ref[...] + _dot(mix, wo_ref[...])
    h = _rms_norm(x1, g_ref[...]).astype(BF16)
    acc = x1
    d_ff = wg_ref.shape[1]
    for c in range(d_ff // ff_chunk):
        cs = slice(c * ff_chunk, (c + 1) * ff_chunk)
        gate = _dot(h, wg_ref[:, cs])
        up = _dot(h, wu_ref[:, cs])
        act = (gate / (1.0 + jnp.exp(-gate)) * up).astype(BF16)
        acc = acc + _dot(act, wd_ref[cs, :])
    o_ref[...] = acc


def _post(x, mixes, w_out, g, wg, wu, wd, *, tm, ff_chunk):
    b, s, d = x.shape
    tok = pl.BlockSpec((None, tm, d), lambda i, j: (i, j, 0))
    mix_specs = [pl.BlockSpec((None, tm, m.shape[2]), lambda i, j: (i, j, 0)) for m in mixes]
    return pl.pallas_call(
        functools.partial(_post_kernel, n_mix=len(mixes), ff_chunk=ff_chunk),
        grid=(b, s // tm),
        in_specs=[tok] + mix_specs
                 + [_full_spec(w_out), _full_spec(g), _full_spec(wg), _full_spec(wu), _full_spec(wd)],
        out_specs=tok,
        out_shape=jax.ShapeDtypeStruct(x.shape, x.dtype),
        compiler_params=_params(),
        name="outproj_ffn",
    )(x, *mixes, w_out, g, wg, wu, wd)


def _rope(xr, cosv, sin_lo, sin_hi):
    half = QK_ROPE // 2
    return (xr * cosv + pltpu.roll(xr, LANES - half, 1) * sin_lo
            + pltpu.roll(xr, half, 1) * sin_hi)


def _odd_pre_kernel(x_ref, pos_ref, g_ref, w_in_ref, pool_w_ref, pool_scale_ref,
                    qa_g_ref, qb_ref, kva_g_ref, kvb_ref, qg_n_ref, qg_r_ref,
                    kg_n_ref, kg_r_ref, invf_ref,
                    c_ref, q_ref, k_ref, v_ref, zbuf_ref, *, tm):
    pool_w = len(POOL_WINDOWS) * POOL_GROUP_DIM
    q_lora = qa_g_ref.shape[1]
    kv_lora = kva_g_ref.shape[1]
    h = _rms_norm(x_ref[...], g_ref[...]).astype(BF16)
    proj = _dot(h, w_in_ref[...])
    zp = proj[:, 0:pool_w]
    q_lat = proj[:, pool_w:pool_w + q_lora]
    kv_lat = proj[:, pool_w + q_lora:pool_w + q_lora + kv_lora]
    kr = proj[:, pool_w + q_lora + kv_lora:]

    @pl.when(pl.program_id(1) == 0)
    def _():
        zbuf_ref[0:POOL_CARRY_ROWS, :] = jnp.zeros((POOL_CARRY_ROWS, pool_w), F32)

    zbuf_ref[POOL_CARRY_ROWS:POOL_CARRY_ROWS + tm, :] = zp
    lane = lax.broadcasted_iota(jnp.int32, (tm, LANES), 1)
    t1 = (pl.program_id(1) * tm + 1
          + lax.broadcasted_iota(jnp.int32, (tm, 1), 0)).astype(F32)
    pooled = []
    for half_idx in range(pool_w // LANES):
        cs = slice(half_idx * LANES, (half_idx + 1) * LANES)
        w_lo, w_hi = POOL_WINDOWS[2 * half_idx], POOL_WINDOWS[2 * half_idx + 1]
        acc = zp[:, cs]
        sums = {}
        for sh in range(1, w_hi):
            acc = acc + zbuf_ref[POOL_CARRY_ROWS - sh:POOL_CARRY_ROWS - sh + tm, cs]
            if sh + 1 in (w_lo, w_hi):
                sums[sh + 1] = acc
        mean_lo = sums[w_lo] / jnp.minimum(t1, float(w_lo))
        mean_hi = sums[w_hi] / jnp.minimum(t1, float(w_hi))
        pooled.append(jnp.where(lane < POOL_GROUP_DIM, mean_lo, mean_hi) - zp[:, cs])
    zbuf_ref[0:POOL_CARRY_ROWS, :] = zbuf_ref[tm:tm + POOL_CARRY_ROWS, :]
    pooled = jnp.concatenate(pooled, axis=1).astype(BF16)
    c_ref[...] = (_dot(pooled, pool_w_ref[...]) * pool_scale_ref[...]).astype(BF16)

    ang = pos_ref[...].astype(F32) * invf_ref[...]
    cosv = jnp.cos(ang)
    sinv = jnp.sin(ang)
    half = QK_ROPE // 2
    sin_lo = jnp.where(lane < half, -sinv, 0.0)
    sin_hi = jnp.where((lane >= half) & (lane < QK_ROPE), sinv, 0.0)

    qn = _rms_norm(q_lat, qa_g_ref[...]).astype(BF16)
    q = _dot(qn, qb_ref[...])
    kvn = _rms_norm(kv_lat, kva_g_ref[...]).astype(BF16)
    kv = _dot(kvn, kvb_ref[...])
    scale = QK_DIM ** -0.5
    nope_w = MLA_HEADS * QK_NOPE
    kr_ss = jnp.sum(kr * kr, axis=-1, keepdims=True)
    for hd in range(MLA_HEADS):
        q_n = q[:, hd * QK_NOPE:(hd + 1) * QK_NOPE]
        q_r = q[:, nope_w + hd * LANES:nope_w + (hd + 1) * LANES]
        ss = jnp.sum(q_n * q_n, axis=-1, keepdims=True) + jnp.sum(q_r * q_r, axis=-1, keepdims=True)
        rinv = lax.rsqrt(ss * (1.0 / QK_DIM) + EPS)
        q_ref[hd, :, 0:QK_NOPE] = (q_n * rinv * qg_n_ref[...] * scale).astype(BF16)
        q_rot = _rope(q_r * rinv * qg_r_ref[...], cosv, sin_lo, sin_hi)
        q_ref[hd, :, QK_NOPE:] = (q_rot * scale).astype(BF16)

        k_n = kv[:, hd * (QK_NOPE + V_DIM):hd * (QK_NOPE + V_DIM) + QK_NOPE]
        ss = jnp.sum(k_n * k_n, axis=-1, keepdims=True) + kr_ss
        rinv = lax.rsqrt(ss * (1.0 / QK_DIM) + EPS)
        k_ref[hd, :, 0:QK_NOPE] = (k_n * rinv * kg_n_ref[...]).astype(BF16)
        k_ref[hd, :, QK_NOPE:] = _rope(kr * rinv * kg_r_ref[...], cosv, sin_lo, sin_hi).astype(BF16)
        v_ref[hd] = kv[:, hd * (QK_NOPE + V_DIM) + QK_NOPE:(hd + 1) * (QK_NOPE + V_DIM)].astype(BF16)


def _odd_pre(x, pos, g, w_in, pool_w, pool_scale, qa_g, qb, kva_g, kvb,
             qg_n, qg_r, kg_n, kg_r, invf, *, tm):
    b, s, d = x.shape
    pool_width = pool_w.shape[0]
    qk_pad = QK_NOPE + LANES
    tok = pl.BlockSpec((None, tm, d), lambda i, j: (i, j, 0))
    consts = [g, w_in, pool_w, pool_scale, qa_g, qb, kva_g, kvb, qg_n, qg_r, kg_n, kg_r, invf]
    head_spec = lambda w: pl.BlockSpec((None, MLA_HEADS, tm, w), lambda i, j: (i, 0, j, 0))
    return pl.pallas_call(
        functools.partial(_odd_pre_kernel, tm=tm),
        grid=(b, s // tm),
        in_specs=[tok, pl.BlockSpec((None, tm, 1), lambda i, j: (i, j, 0))]
                 + [_full_spec(c) for c in consts],
        out_specs=[pl.BlockSpec((None, tm, pool_width), lambda i, j: (i, j, 0)),
                   head_spec(qk_pad), head_spec(qk_pad), head_spec(V_DIM)],
        out_shape=[jax.ShapeDtypeStruct((b, s, pool_width), BF16),
                   jax.ShapeDtypeStruct((b, MLA_HEADS, s, qk_pad), BF16),
                   jax.ShapeDtypeStruct((b, MLA_HEADS, s, qk_pad), BF16),
                   jax.ShapeDtypeStruct((b, MLA_HEADS, s, V_DIM), BF16)],
        scratch_shapes=[pltpu.VMEM((tm + POOL_CARRY_ROWS, pool_width), F32)],
        compiler_params=_params(),
        name="odd_pre",
    )(x, pos, *consts)


def _attn_kernel(q_ref, k_ref, v_ref, o_ref, *, tq):
    s_len = q_ref.shape[0]
    row = lax.broadcasted_iota(jnp.int32, (tq, tq), 0)
    col = lax.broadcasted_iota(jnp.int32, (tq, tq), 1)
    causal = row >= col

    def step(q, k0, carry, masked):
        m, l, acc = carry
        k = k_ref[pl.ds(k0, tq), :]
        v = v_ref[pl.ds(k0, tq), :]
        sc = lax.dot_general(q, k, (((1,), (1,)), ((), ())), preferred_element_type=F32)
        if masked:
            sc = jnp.where(causal, sc, MASK_VALUE)
        m_new = jnp.maximum(m, jnp.max(sc, axis=-1, keepdims=True))
        alpha = jnp.exp(m - m_new)
        p = jnp.exp(sc - m_new)
        l = alpha * l + jnp.sum(p, axis=-1, keepdims=True)
        acc = alpha * acc + _dot(p.astype(BF16), v)
        return m_new, l, acc

    def q_body(qi, _):
        q0 = pl.multiple_of(qi * tq, tq)
        q = q_ref[pl.ds(q0, tq), :]
        init = (jnp.full((tq, 1), MASK_VALUE, F32), jnp.zeros((tq, 1), F32),
                jnp.zeros((tq, V_DIM), F32))
        carry = lax.fori_loop(
            0, qi, lambda kj, c: step(q, pl.multiple_of(kj * tq, tq), c, False), init)
        m, l, acc = step(q, q0, carry, True)
        o_ref[pl.ds(q0, tq), :] = (acc / l).astype(BF16)
        return 0

    lax.fori_loop(0, s_len // tq, q_body, 0)


def _attention(q, k, v, *, tq):
    b, nh, s, qk_pad = q.shape
    qk_spec = pl.BlockSpec((None, None, s, qk_pad), lambda i, j: (i, j, 0, 0))
    return pl.pallas_call(
        functools.partial(_attn_kernel, tq=tq),
        grid=(b, nh),
        in_specs=[qk_spec, qk_spec,
                  pl.BlockSpec((None, None, s, V_DIM), lambda i, j: (i, j, 0, 0))],
        out_specs=pl.BlockSpec((None, s, V_DIM), lambda i, j: (i, 0, j)),
        out_shape=jax.ShapeDtypeStruct((b, s, nh * V_DIM), BF16),
        compiler_params=_params(),
        name="mla_attention",
    )(q, k, v)


def _pad_lanes(a, width):
    return jnp.pad(a, [(0, 0)] * (a.ndim - 1) + [(0, width - a.shape[-1])])


def kernel(x, positions, mix_norm, ffn_norm, even_w_in, sg_ln_g, sg_w_s, sg_b_s, sc_conv_w,
           even_w_out, odd_w_in, pool_w, pool_scale, q_a_norm, q_b, kv_a_norm, kv_b, q_norm,
           k_norm, odd_w_out, ffn_w_gate, ffn_w_up, ffn_w_down):
    depth = mix_norm.shape[0]
    d_model = x.shape[-1]
    tm = 256
    ff_chunk = ffn_w_gate.shape[-1] // 2
    row = lambda a: a.reshape(1, -1)

    pos = positions[..., None]
    half = QK_ROPE // 2
    inv_freq = ROPE_THETA ** (-jnp.arange(0, QK_ROPE, 2, dtype=F32) / QK_ROPE)
    invf = _pad_lanes(jnp.concatenate([inv_freq, inv_freq]), LANES).reshape(1, LANES)
    del half

    for layer in range(depth):
        i = layer // 2
        if layer % 2 == 0:
            mix = _even_mixer(
                x, row(mix_norm[layer]), even_w_in[i].astype(BF16), row(sg_ln_g[i]),
                sg_w_s[i], sg_b_s[i].T, sc_conv_w[i], tm=tm)
            mixes, w_out = [mix], even_w_out[i].astype(BF16)
        else:
            pool_width = pool_scale.shape[-1]
            n_groups = pool_w.shape[1]
            gd = pool_w.shape[2]
            w_bd = jnp.zeros((pool_width, pool_width), F32)
            for gidx in range(n_groups):
                w_bd = w_bd.at[gidx * gd:(gidx + 1) * gd, gidx * gd:(gidx + 1) * gd].set(pool_w[i, gidx])
            q_lora = q_b.shape[1]
            qb3 = q_b[i].reshape(q_lora, MLA_HEADS, QK_DIM)
            qb_nope = qb3[:, :, :QK_NOPE].reshape(q_lora, MLA_HEADS * QK_NOPE)
            qb_rope = _pad_lanes(qb3[:, :, QK_NOPE:], LANES).reshape(q_lora, MLA_HEADS * LANES)
            qb = jnp.concatenate([qb_nope, qb_rope], axis=1).astype(BF16)
            w_in = _pad_lanes(odd_w_in[i], d_model).astype(BF16)
            c_out, q, k, v = _odd_pre(
                x, pos, row(mix_norm[layer]), w_in, w_bd.astype(BF16), row(pool_scale[i]),
                row(q_a_norm[i]), qb, row(kv_a_norm[i]), kv_b[i].astype(BF16),
                row(q_norm[i, :QK_NOPE]), _pad_lanes(row(q_norm[i, QK_NOPE:]), LANES),
                row(k_norm[i, :QK_NOPE]), _pad_lanes(row(k_norm[i, QK_NOPE:]), LANES),
                invf, tm=tm)
            d_out = _attention(q, k, v, tq=256)
            mixes, w_out = [c_out, d_out], odd_w_out[i].astype(BF16)
        x = _post(x, mixes, w_out, row(ffn_norm[layer]), ffn_w_gate[layer].astype(BF16),
                  ffn_w_up[layer].astype(BF16), ffn_w_down[layer].astype(BF16),
                  tm=tm, ff_chunk=ff_chunk)
    return x
```

```python
import functools
import math

import jax
import jax.numpy as jnp
from jax import lax
from jax.experimental import pallas as pl
from jax.experimental.pallas import tpu as pltpu

F32 = jnp.float32
BF16 = jnp.bfloat16

EPS = 1e-6
MASK_VALUE = -1e30
LANES = 128
SG_CHUNK = 128
SG_HEADS = 4
CONV_WIDTH = 3
POOL_WINDOWS = (2, 4, 8, 16)
POOL_GROUP_DIM = 64
MLA_HEADS = 6
QK_NOPE = 128
QK_ROPE = 64
QK_DIM = QK_NOPE + QK_ROPE
V_DIM = 128
ROPE_THETA = 10000.0
CARRY_ROWS = 8
POOL_CARRY_ROWS = 16
VMEM_LIMIT_BYTES = 56 * 1024 * 1024


def _rms_norm(x, g):
    ms = jnp.mean(x * x, axis=-1, keepdims=True)
    return x * lax.rsqrt(ms + EPS) * g


def _gelu(x):
    return 0.5 * x * (1.0 + lax.erf(x * math.sqrt(0.5)))


def _dot(a, b):
    return jnp.dot(a, b, preferred_element_type=F32)


def _full_spec(arr):
    nd = arr.ndim
    return pl.BlockSpec(arr.shape, lambda *_: (0,) * nd)


def _params():
    return pltpu.CompilerParams(
        dimension_semantics=("arbitrary", "arbitrary"),
        vmem_limit_bytes=VMEM_LIMIT_BYTES)


def _even_mixer_kernel(x_ref, g_ref, w_in_ref, ln_g_ref, w_s_ref, b_st_ref,
                       conv_w_ref, mix_ref, zbuf_ref, *, tm):
    sg_w = SG_HEADS * LANES
    h = _rms_norm(x_ref[...], g_ref[...]).astype(BF16)
    proj = _dot(h, w_in_ref[...])
    u = _gelu(proj[:, 0:sg_w])
    v = _gelu(proj[:, sg_w:2 * sg_w])
    sc_w = (proj.shape[1] - 2 * sg_w) // 3
    b_gate = proj[:, 2 * sg_w:2 * sg_w + sc_w]
    c_gate = proj[:, 2 * sg_w + sc_w:2 * sg_w + 2 * sc_w]
    hv = proj[:, 2 * sg_w + 2 * sc_w:]

    row = lax.broadcasted_iota(jnp.int32, (SG_CHUNK, SG_CHUNK), 0)
    col = lax.broadcasted_iota(jnp.int32, (SG_CHUNK, SG_CHUNK), 1)
    causal = row >= col
    for hd in range(SG_HEADS):
        cs = slice(hd * LANES, (hd + 1) * LANES)
        vh = v[:, cs]
        mu = jnp.mean(vh, axis=-1, keepdims=True)
        xc = vh - mu
        var = jnp.mean(xc * xc, axis=-1, keepdims=True)
        vn = (xc * lax.rsqrt(var + EPS) * ln_g_ref[:, cs]).astype(BF16)
        w = jnp.where(causal, w_s_ref[hd], 0.0).astype(BF16)
        bias = b_st_ref[:, hd:hd + 1]
        for c in range(tm // SG_CHUNK):
            rs = slice(c * SG_CHUNK, (c + 1) * SG_CHUNK)
            mixed = _dot(w, vn[rs]) + bias
            mix_ref[rs, cs] = (u[rs, cs] * mixed).astype(BF16)

    @pl.when(pl.program_id(1) == 0)
    def _():
        zbuf_ref[0:CARRY_ROWS, :] = jnp.zeros((CARRY_ROWS, sc_w), F32)

    z = c_gate * hv
    zbuf_ref[CARRY_ROWS:CARRY_ROWS + tm, :] = z
    y = conv_w_ref[CONV_WIDTH - 1:CONV_WIDTH, :] * z
    for k in range(CONV_WIDTH - 1):
        shift = CONV_WIDTH - 1 - k
        y = y + conv_w_ref[k:k + 1, :] * zbuf_ref[CARRY_ROWS - shift:CARRY_ROWS - shift + tm, :]
    mix_ref[:, sg_w:] = (b_gate * y).astype(BF16)
    zbuf_ref[0:CARRY_ROWS, :] = zbuf_ref[tm:tm + CARRY_ROWS, :]


def _even_mixer(x, g, w_in, ln_g, w_s, b_st, conv_w, *, tm):
    b, s, d = x.shape
    sc_w = conv_w.shape[1]
    mix_w = SG_HEADS * LANES + sc_w
    tok = pl.BlockSpec((None, tm, d), lambda i, j: (i, j, 0))
    return pl.pallas_call(
        functools.partial(_even_mixer_kernel, tm=tm),
        grid=(b, s // tm),
        in_specs=[tok, _full_spec(g), _full_spec(w_in), _full_spec(ln_g),
                  _full_spec(w_s), _full_spec(b_st), _full_spec(conv_w)],
        out_specs=pl.BlockSpec((None, tm, mix_w), lambda i, j: (i, j, 0)),
        out_shape=jax.ShapeDtypeStruct((b, s, mix_w), BF16),
        scratch_shapes=[pltpu.VMEM((tm + CARRY_ROWS, sc_w), F32)],
        compiler_params=_params(),
        name="even_mixer",
    )(x, g, w_in, ln_g, w_s, b_st, conv_w)


def _post_kernel(*refs, n_mix, ff_chunk):
    x_ref = refs[0]
    mix_refs = refs[1:1 + n_mix]
    wo_ref, g_ref, wg_ref, wu_ref, wd_ref, o_ref = refs[1 + n_mix:]
    mix = [m_ref[...] for m_ref in mix_refs]
    mix = mix[0] if n_mix == 1 else jnp.concatenate(mix, axis=1)
    x1 = x_ref[...] + _dot(mix, wo_ref[...])
    h = _rms_norm(x1, g_ref[...]).astype(BF16)
    acc = x1
    d_ff = wg_ref.shape[1]
    for c in range(d_ff // ff_chunk):
        cs = slice(c * ff_chunk, (c + 1) * ff_chunk)
        gate = _dot(h, wg_ref[:, cs])
        up = _dot(h, wu_ref[:, cs])
        act = (gate / (1.0 + jnp.exp(-gate)) * up).astype(BF16)
        acc = acc + _dot(act, wd_ref[cs, :])
    o_ref[...] = acc


def _post(x, mixes, w_out, g, wg, wu, wd, *, tm, ff_chunk):
    b, s, d = x.shape
    tok = pl.BlockSpec((None, tm, d), lambda i, j: (i, j, 0))
    mix_specs = [pl.BlockSpec((None, tm, m.shape[2]), lambda i, j: (i, j, 0)) for m in mixes]
    return pl.pallas_call(
        functools.partial(_post_kernel, n_mix=len(mixes), ff_chunk=ff_chunk),
        grid=(b, s // tm),
        in_specs=[tok] + mix_specs
                 + [_full_spec(w_out), _full_spec(g), _full_spec(wg), _full_spec(wu), _full_spec(wd)],
        out_specs=tok,
        out_shape=jax.ShapeDtypeStruct(x.shape, x.dtype),
        compiler_params=_params(),
        name="outproj_ffn",
    )(x, *mixes, w_out, g, wg, wu, wd)


def _rope(xr, cosv, sin_lo, sin_hi):
    half = QK_ROPE // 2
    return (xr * cosv + pltpu.roll(xr, LANES - half, 1) * sin_lo
            + pltpu.roll(xr, half, 1) * sin_hi)


def _odd_pre_kernel(x_ref, pos_ref, g_ref, w_in_ref, pool_w_ref, pool_scale_ref,
                    qa_g_ref, qb_ref, kva_g_ref, kvb_ref, qg_n_ref, qg_r_ref,
                    kg_n_ref, kg_r_ref, invf_ref,
                    c_ref, q_ref, k_ref, v_ref, zbuf_ref, *, tm):
    pool_w = len(POOL_WINDOWS) * POOL_GROUP_DIM
    q_lora = qa_g_ref.shape[1]
    kv_lora = kva_g_ref.shape[1]
    h = _rms_norm(x_ref[...], g_ref[...]).astype(BF16)
    proj = _dot(h, w_in_ref[...])
    zp = proj[:, 0:pool_w]
    q_lat = proj[:, pool_w:pool_w + q_lora]
    kv_lat = proj[:, pool_w + q_lora:pool_w + q_lora + kv_lora]
    kr = proj[:, pool_w + q_lora + kv_lora:]

    @pl.when(pl.program_id(1) == 0)
    def _():
        zbuf_ref[0:POOL_CARRY_ROWS, :] = jnp.zeros((POOL_CARRY_ROWS, pool_w), F32)

    zbuf_ref[POOL_CARRY_ROWS:POOL_CARRY_ROWS + tm, :] = zp
    lane = lax.broadcasted_iota(jnp.int32, (tm, LANES), 1)
    t1 = (pl.program_id(1) * tm + 1
          + lax.broadcasted_iota(jnp.int32, (tm, 1), 0)).astype(F32)
    pooled = []
    for half_idx in range(pool_w // LANES):
        cs = slice(half_idx * LANES, (half_idx + 1) * LANES)
        w_lo, w_hi = POOL_WINDOWS[2 * half_idx], POOL_WINDOWS[2 * half_idx + 1]
        acc = zp[:, cs]
        sums = {}
        for sh in range(1, w_hi):
            acc = acc + zbuf_ref[POOL_CARRY_ROWS - sh:POOL_CARRY_ROWS - sh + tm, cs]
            if sh + 1 in (w_lo, w_hi):
                sums[sh + 1] = acc
        mean_lo = sums[w_lo] / jnp.minimum(t1, float(w_lo))
        mean_hi = sums[w_hi] / jnp.minimum(t1, float(w_hi))
        pooled.append(jnp.where(lane < POOL_GROUP_DIM, mean_lo, mean_hi) - zp[:, cs])
    zbuf_ref[0:POOL_CARRY_ROWS, :] = zbuf_ref[tm:tm + POOL_CARRY_ROWS, :]
    pooled = jnp.concatenate(pooled, axis=1).astype(BF16)
    c_ref[...] = (_dot(pooled, pool_w_ref[...]) * pool_scale_ref[...]).astype(BF16)

    ang = pos_ref[...].astype(F32) * invf_ref[...]
    cosv = jnp.cos(ang)
    sinv = jnp.sin(ang)
    half = QK_ROPE // 2
    sin_lo = jnp.where(lane < half, -sinv, 0.0)
    sin_hi = jnp.where((lane >= half) & (lane < QK_ROPE), sinv, 0.0)

    qn = _rms_norm(q_lat, qa_g_ref[...]).astype(BF16)
    q = _dot(qn, qb_ref[...])
    kvn = _rms_norm(kv_lat, kva_g_ref[...]).astype(BF16)
    kv = _dot(kvn, kvb_ref[...])
    scale = QK_DIM ** -0.5 * math.log2(math.e)
    nope_w = MLA_HEADS * QK_NOPE
    kr_ss = jnp.sum(kr * kr, axis=-1, keepdims=True)
    for hd in range(MLA_HEADS):
        q_n = q[:, hd * QK_NOPE:(hd + 1) * QK_NOPE]
        q_r = q[:, nope_w + hd * LANES:nope_w + (hd + 1) * LANES]
        ss = jnp.sum(q_n * q_n, axis=-1, keepdims=True) + jnp.sum(q_r * q_r, axis=-1, keepdims=True)
        rinv = lax.rsqrt(ss * (1.0 / QK_DIM) + EPS)
        q_ref[hd, :, 0:QK_NOPE] = (q_n * rinv * qg_n_ref[...] * scale).astype(BF16)
        q_rot = _rope(q_r * rinv * qg_r_ref[...], cosv, sin_lo, sin_hi)
        q_ref[hd, :, QK_NOPE:] = (q_rot * scale).astype(BF16)

        k_n = kv[:, hd * (QK_NOPE + V_DIM):hd * (QK_NOPE + V_DIM) + QK_NOPE]
        ss = jnp.sum(k_n * k_n, axis=-1, keepdims=True) + kr_ss
        rinv = lax.rsqrt(ss * (1.0 / QK_DIM) + EPS)
        k_ref[hd, :, 0:QK_NOPE] = (k_n * rinv * kg_n_ref[...]).astype(BF16)
        k_ref[hd, :, QK_NOPE:] = _rope(kr * rinv * kg_r_ref[...], cosv, sin_lo, sin_hi).astype(BF16)
        v_ref[hd] = kv[:, hd * (QK_NOPE + V_DIM) + QK_NOPE:(hd + 1) * (QK_NOPE + V_DIM)].astype(BF16)


def _odd_pre(x, pos, g, w_in, pool_w, pool_scale, qa_g, qb, kva_g, kvb,
             qg_n, qg_r, kg_n, kg_r, invf, *, tm):
    b, s, d = x.shape
    pool_width = pool_w.shape[0]
    qk_pad = QK_NOPE + LANES
    tok = pl.BlockSpec((None, tm, d), lambda i, j: (i, j, 0))
    consts = [g, w_in, pool_w, pool_scale, qa_g, qb, kva_g, kvb, qg_n, qg_r, kg_n, kg_r, invf]
    head_spec = lambda w: pl.BlockSpec((None, MLA_HEADS, tm, w), lambda i, j: (i, 0, j, 0))
    return pl.pallas_call(
        functools.partial(_odd_pre_kernel, tm=tm),
        grid=(b, s // tm),
        in_specs=[tok, pl.BlockSpec((None, tm, 1), lambda i, j: (i, j, 0))]
                 + [_full_spec(c) for c in consts],
        out_specs=[pl.BlockSpec((None, tm, pool_width), lambda i, j: (i, j, 0)),
                   head_spec(qk_pad), head_spec(qk_pad), head_spec(V_DIM)],
        out_shape=[jax.ShapeDtypeStruct((b, s, pool_width), BF16),
                   jax.ShapeDtypeStruct((b, MLA_HEADS, s, qk_pad), BF16),
                   jax.ShapeDtypeStruct((b, MLA_HEADS, s, qk_pad), BF16),
                   jax.ShapeDtypeStruct((b, MLA_HEADS, s, V_DIM), BF16)],
        scratch_shapes=[pltpu.VMEM((tm + POOL_CARRY_ROWS, pool_width), F32)],
        compiler_params=_params(),
        name="odd_pre",
    )(x, pos, *consts)


def _attn_kernel(q_ref, k_ref, v_ref, o_ref, m_ref, l_ref, acc_ref, *, tq, hp):
    s_len = q_ref.shape[1]
    n_col = tq // LANES
    row = lax.broadcasted_iota(jnp.int32, (tq, tq), 0)
    col = lax.broadcasted_iota(jnp.int32, (tq, tq), 1)
    causal = row >= col

    def tile(hd, q0, k0, masked):
        q = q_ref[hd, pl.ds(q0, tq), :]
        k = k_ref[hd, pl.ds(k0, tq), :]
        v = v_ref[hd, pl.ds(k0, tq), :]
        sc = lax.dot_general(q, k, (((1,), (1,)), ((), ())), preferred_element_type=F32)
        if masked:
            sc = jnp.where(causal, sc, MASK_VALUE)
        cols = [sc[:, c * LANES:(c + 1) * LANES] for c in range(n_col)]
        cmax = functools.reduce(jnp.maximum, cols)
        m_prev = m_ref[hd]
        m_new = jnp.maximum(m_prev, jnp.max(cmax, axis=-1, keepdims=True))
        alpha = jnp.exp2(m_prev - m_new)
        ps = [jnp.exp2(c - m_new) for c in cols]
        l_ref[hd] = alpha * l_ref[hd] + functools.reduce(jnp.add, ps)
        p = jnp.concatenate(ps, axis=1).astype(BF16)
        acc_ref[hd] = alpha * acc_ref[hd] + _dot(p, v)
        m_ref[hd] = m_new

    def q_body(qi, _):
        q0 = pl.multiple_of(qi * tq, tq)
        m_ref[...] = jnp.full(m_ref.shape, MASK_VALUE, F32)
        l_ref[...] = jnp.zeros(l_ref.shape, F32)
        acc_ref[...] = jnp.zeros(acc_ref.shape, F32)

        def k_body(kj, _):
            k0 = pl.multiple_of(kj * tq, tq)
            for hd in range(hp):
                tile(hd, q0, k0, False)
            return 0

        lax.fori_loop(0, qi, k_body, 0)
        for hd in range(hp):
            tile(hd, q0, q0, True)
        for hd in range(hp):
            l = jnp.sum(l_ref[hd], axis=-1, keepdims=True)
            o_ref[pl.ds(q0, tq), hd * V_DIM:(hd + 1) * V_DIM] = (acc_ref[hd] / l).astype(BF16)
        return 0

    lax.fori_loop(0, s_len // tq, q_body, 0)


def _attention(q, k, v, *, tq, hp):
    b, nh, s, qk_pad = q.shape
    qk_spec = pl.BlockSpec((None, hp, s, qk_pad), lambda i, j: (i, j, 0, 0))
    return pl.pallas_call(
        functools.partial(_attn_kernel, tq=tq, hp=hp),
        grid=(b, nh // hp),
        in_specs=[qk_spec, qk_spec,
                  pl.BlockSpec((None, hp, s, V_DIM), lambda i, j: (i, j, 0, 0))],
        out_specs=pl.BlockSpec((None, s, hp * V_DIM), lambda i, j: (i, 0, j)),
        out_shape=jax.ShapeDtypeStruct((b, s, nh * V_DIM), BF16),
        scratch_shapes=[pltpu.VMEM((hp, tq, LANES), F32), pltpu.VMEM((hp, tq, LANES), F32),
                        pltpu.VMEM((hp, tq, V_DIM), F32)],
        compiler_params=_params(),
        name="mla_attention",
    )(q, k, v)


def _pad_lanes(a, width):
    return jnp.pad(a, [(0, 0)] * (a.ndim - 1) + [(0, width - a.shape[-1])])


def kernel(x, positions, mix_norm, ffn_norm, even_w_in, sg_ln_g, sg_w_s, sg_b_s, sc_conv_w,
           even_w_out, odd_w_in, pool_w, pool_scale, q_a_norm, q_b, kv_a_norm, kv_b, q_norm,
           k_norm, odd_w_out, ffn_w_gate, ffn_w_up, ffn_w_down):
    depth = mix_norm.shape[0]
    d_model = x.shape[-1]
    tm = 256
    ff_chunk = ffn_w_gate.shape[-1] // 2
    row = lambda a: a.reshape(1, -1)

    pos = positions[..., None]
    half = QK_ROPE // 2
    inv_freq = ROPE_THETA ** (-jnp.arange(0, QK_ROPE, 2, dtype=F32) / QK_ROPE)
    invf = _pad_lanes(jnp.concatenate([inv_freq, inv_freq]), LANES).reshape(1, LANES)
    del half

    for layer in range(depth):
        i = layer // 2
        if layer % 2 == 0:
            mix = _even_mixer(
                x, row(mix_norm[layer]), even_w_in[i].astype(BF16), row(sg_ln_g[i]),
                sg_w_s[i], sg_b_s[i].T, sc_conv_w[i], tm=tm)
            mixes, w_out = [mix], even_w_out[i].astype(BF16)
        else:
            pool_width = pool_scale.shape[-1]
            n_groups = pool_w.shape[1]
            gd = pool_w.shape[2]
            w_bd = jnp.zeros((pool_width, pool_width), F32)
            for gidx in range(n_groups):
                w_bd = w_bd.at[gidx * gd:(gidx + 1) * gd, gidx * gd:(gidx + 1) * gd].set(pool_w[i, gidx])
            q_lora = q_b.shape[1]
            qb3 = q_b[i].reshape(q_lora, MLA_HEADS, QK_DIM)
            qb_nope = qb3[:, :, :QK_NOPE].reshape(q_lora, MLA_HEADS * QK_NOPE)
            qb_rope = _pad_lanes(qb3[:, :, QK_NOPE:], LANES).reshape(q_lora, MLA_HEADS * LANES)
            qb = jnp.concatenate([qb_nope, qb_rope], axis=1).astype(BF16)
            w_in = _pad_lanes(odd_w_in[i], d_model).astype(BF16)
            c_out, q, k, v = _odd_pre(
                x, pos, row(mix_norm[layer]), w_in, w_bd.astype(BF16), row(pool_scale[i]),
                row(q_a_norm[i]), qb, row(kv_a_norm[i]), kv_b[i].astype(BF16),
                row(q_norm[i, :QK_NOPE]), _pad_lanes(row(q_norm[i, QK_NOPE:]), LANES),
                row(k_norm[i, :QK_NOPE]), _pad_lanes(row(k_norm[i, QK_NOPE:]), LANES),
                invf, tm=tm)
            d_out = _attention(q, k, v, tq=512, hp=2)
            mixes, w_out = [c_out, d_out], odd_w_out[i].astype(BF16)
        x = _post(x, mixes, w_out, row(ffn_norm[layer]), ffn_w_gate[layer].astype(BF16),
                  ffn_w_up[layer].astype(BF16), ffn_w_down[layer].astype(BF16),
                  tm=tm, ff_chunk=ff_chunk)
    return x
```

```python
import functools
import math

import jax
import jax.numpy as jnp
from jax import lax
from jax.experimental import pallas as pl
from jax.experimental.pallas import tpu as pltpu

F32 = jnp.float32
BF16 = jnp.bfloat16

EPS = 1e-6
MASK_VALUE = -1e30
LANES = 128
MXU_WIDTH = 256
SG_CHUNK = 128
SG_HEADS = 4
CONV_WIDTH = 3
POOL_WINDOWS = (2, 4, 8, 16)
POOL_GROUP_DIM = 64
MLA_HEADS = 6
QK_NOPE = 128
QK_ROPE = 64
QK_DIM = QK_NOPE + QK_ROPE
V_DIM = 128
ROPE_THETA = 10000.0
CARRY_ROWS = 8
POOL_CARRY_ROWS = 16
VMEM_LIMIT_BYTES = 56 * 1024 * 1024


def _rms_norm(x, g):
    ms = jnp.mean(x * x, axis=-1, keepdims=True)
    return x * lax.rsqrt(ms + EPS) * g


def _gelu(x):
    return 0.5 * x * (1.0 + lax.erf(x * math.sqrt(0.5)))


def _dot(a, b):
    return jnp.dot(a, b, preferred_element_type=F32)


def _full_spec(arr):
    nd = arr.ndim
    return pl.BlockSpec(arr.shape, lambda *_: (0,) * nd)


def _params():
    return pltpu.CompilerParams(
        dimension_semantics=("arbitrary", "arbitrary"),
        vmem_limit_bytes=VMEM_LIMIT_BYTES)


def _even_mixer_kernel(x_ref, g_ref, w_in_ref, ln_g_ref, w_s_ref, b_st_ref,
                       conv_w_ref, mix_ref, zbuf_ref, *, tm):
    sg_w = SG_HEADS * LANES
    h = _rms_norm(x_ref[...], g_ref[...]).astype(BF16)
    proj = _dot(h, w_in_ref[...])
    u = _gelu(proj[:, 0:sg_w])
    v = _gelu(proj[:, sg_w:2 * sg_w])
    sc_w = (proj.shape[1] - 2 * sg_w) // 3
    b_gate = proj[:, 2 * sg_w:2 * sg_w + sc_w]
    c_gate = proj[:, 2 * sg_w + sc_w:2 * sg_w + 2 * sc_w]
    hv = proj[:, 2 * sg_w + 2 * sc_w:]

    row = lax.broadcasted_iota(jnp.int32, (SG_CHUNK, SG_CHUNK), 0)
    col = lax.broadcasted_iota(jnp.int32, (SG_CHUNK, SG_CHUNK), 1)
    causal = row >= col
    for hd in range(SG_HEADS):
        cs = slice(hd * LANES, (hd + 1) * LANES)
        vh = v[:, cs]
        mu = jnp.mean(vh, axis=-1, keepdims=True)
        xc = vh - mu
        var = jnp.mean(xc * xc, axis=-1, keepdims=True)
        vn = (xc * lax.rsqrt(var + EPS) * ln_g_ref[:, cs]).astype(BF16)
        w = jnp.where(causal, w_s_ref[hd], 0.0).astype(BF16)
        bias = b_st_ref[:, hd:hd + 1]
        for c in range(tm // SG_CHUNK):
            rs = slice(c * SG_CHUNK, (c + 1) * SG_CHUNK)
            mixed = _dot(w, vn[rs]) + bias
            mix_ref[rs, cs] = (u[rs, cs] * mixed).astype(BF16)

    @pl.when(pl.program_id(1) == 0)
    def _():
        zbuf_ref[0:CARRY_ROWS, :] = jnp.zeros((CARRY_ROWS, sc_w), F32)

    z = c_gate * hv
    zbuf_ref[CARRY_ROWS:CARRY_ROWS + tm, :] = z
    y = conv_w_ref[CONV_WIDTH - 1:CONV_WIDTH, :] * z
    for k in range(CONV_WIDTH - 1):
        shift = CONV_WIDTH - 1 - k
        y = y + conv_w_ref[k:k + 1, :] * zbuf_ref[CARRY_ROWS - shift:CARRY_ROWS - shift + tm, :]
    mix_ref[:, sg_w:] = (b_gate * y).astype(BF16)
    zbuf_ref[0:CARRY_ROWS, :] = zbuf_ref[tm:tm + CARRY_ROWS, :]


def _even_mixer(x, g, w_in, ln_g, w_s, b_st, conv_w, *, tm):
    b, s, d = x.shape
    sc_w = conv_w.shape[1]
    mix_w = SG_HEADS * LANES + sc_w
    tok = pl.BlockSpec((None, tm, d), lambda i, j: (i, j, 0))
    return pl.pallas_call(
        functools.partial(_even_mixer_kernel, tm=tm),
        grid=(b, s // tm),
        in_specs=[tok, _full_spec(g), _full_spec(w_in), _full_spec(ln_g),
                  _full_spec(w_s), _full_spec(b_st), _full_spec(conv_w)],
        out_specs=pl.BlockSpec((None, tm, mix_w), lambda i, j: (i, j, 0)),
        out_shape=jax.ShapeDtypeStruct((b, s, mix_w), BF16),
        scratch_shapes=[pltpu.VMEM((tm + CARRY_ROWS, sc_w), F32)],
        compiler_params=_params(),
        name="even_mixer",
    )(x, g, w_in, ln_g, w_s, b_st, conv_w)


def _post_kernel(*refs, n_mix, ff_chunk):
    x_ref = refs[0]
    mix_refs = refs[1:1 + n_mix]
    wo_ref, g_ref, wg_ref, wu_ref, wd_ref, o_ref = refs[1 + n_mix:]
    mix = [m_ref[...] for m_ref in mix_refs]
    mix = mix[0] if n_mix == 1 else jnp.concatenate(mix, axis=1)
    x1 = x_ref[...] + _dot(mix, wo_ref[...])
    h = _rms_norm(x1, g_ref[...]).astype(BF16)
    acc = x1
    d_ff = wg_ref.shape[1]
    for c0 in range(0, d_ff, ff_chunk):
        cs = slice(c0, min(c0 + ff_chunk, d_ff))
        gate = _dot(h, wg_ref[:, cs])
        up = _dot(h, wu_ref[:, cs])
        act = (gate / (1.0 + jnp.exp(-gate)) * up).astype(BF16)
        acc = acc + _dot(act, wd_ref[cs, :])
    o_ref[...] = acc


def _post(x, mixes, w_out, g, wg, wu, wd, *, tm, ff_chunk):
    b, s, d = x.shape
    tok = pl.BlockSpec((None, tm, d), lambda i, j: (i, j, 0))
    mix_specs = [pl.BlockSpec((None, tm, m.shape[2]), lambda i, j: (i, j, 0)) for m in mixes]
    return pl.pallas_call(
        functools.partial(_post_kernel, n_mix=len(mixes), ff_chunk=ff_chunk),
        grid=(b, s // tm),
        in_specs=[tok] + mix_specs
                 + [_full_spec(w_out), _full_spec(g), _full_spec(wg), _full_spec(wu), _full_spec(wd)],
        out_specs=tok,
        out_shape=jax.ShapeDtypeStruct(x.shape, x.dtype),
        compiler_params=_params(),
        name="outproj_ffn",
    )(x, *mixes, w_out, g, wg, wu, wd)


def _rope_table_kernel(pos_ref, invf_ref, cos_ref, sin_ref):
    half = QK_ROPE // 2
    rows = pos_ref.shape[0]
    pos = pos_ref[...].astype(F32)
    lane = lax.broadcasted_iota(jnp.int32, (rows, LANES), 1)
    p = pos[:, 0:1]
    for grp in range(1, LANES // half):
        p = jnp.where(lane >= grp * half, pos[:, grp:grp + 1], p)
    ang = p * invf_ref[...]
    cos_ref[...] = jnp.cos(ang)
    sin_ref[...] = jnp.sin(ang)


def _rope_tables(positions, *, rows):
    half = QK_ROPE // 2
    per_row = LANES // half
    b, s = positions.shape
    inv_freq = ROPE_THETA ** (-jnp.arange(0, QK_ROPE, 2, dtype=F32) / QK_ROPE)
    invf = jnp.tile(inv_freq, per_row).reshape(1, LANES)
    pos4 = positions.reshape(b * s // per_row, per_row)
    n = pos4.shape[0]
    tab = jax.ShapeDtypeStruct((n, LANES), F32)
    cos_p, sin_p = pl.pallas_call(
        _rope_table_kernel,
        grid=(n // rows,),
        in_specs=[pl.BlockSpec((rows, per_row), lambda i: (i, 0)), _full_spec(invf)],
        out_specs=[pl.BlockSpec((rows, LANES), lambda i: (i, 0))] * 2,
        out_shape=[tab, tab],
        compiler_params=pltpu.CompilerParams(dimension_semantics=("arbitrary",),
                                             vmem_limit_bytes=VMEM_LIMIT_BYTES),
        name="rope_tables",
    )(pos4, invf)
    return cos_p.reshape(b, s, half), sin_p.reshape(b, s, half)


def _shifted(buf_ref, shift, tm, cs):
    return buf_ref[POOL_CARRY_ROWS - shift:POOL_CARRY_ROWS - shift + tm, cs]


def _odd_pre_kernel(x_ref, cos_ref, sin_ref, g_ref, w_in_ref, pool_w_ref, pool_scale_ref,
                    qa_g_ref, qb_ref, kva_g_ref, kvb_ref, qg_n_ref, qg_r_ref, qg_sw_ref,
                    kg_n_ref, kg_r_ref, kg_sw_ref,
                    c_ref, q_ref, k_ref, v_ref, zbuf_ref, s2buf_ref, s4buf_ref, s8buf_ref, *, tm):
    pool_w = len(POOL_WINDOWS) * POOL_GROUP_DIM
    q_lora = qa_g_ref.shape[1]
    kv_lora = kva_g_ref.shape[1]
    h = _rms_norm(x_ref[...], g_ref[...]).astype(BF16)
    proj = _dot(h, w_in_ref[...])
    zp = proj[:, 0:pool_w]
    q_lat = proj[:, pool_w:pool_w + q_lora]
    kv_lat = proj[:, pool_w + q_lora:pool_w + q_lora + kv_lora]
    kr_off = pool_w + q_lora + kv_lora
    kr = proj[:, kr_off:kr_off + LANES]
    kr_sw = proj[:, kr_off + LANES:kr_off + 2 * LANES]

    lo, hi = slice(0, LANES), slice(LANES, 2 * LANES)
    bufs = (zbuf_ref, s2buf_ref, s4buf_ref, s8buf_ref)

    @pl.when(pl.program_id(1) == 0)
    def _():
        for buf in bufs:
            buf[0:POOL_CARRY_ROWS, :] = jnp.zeros((POOL_CARRY_ROWS, buf.shape[1]), F32)

    body = slice(POOL_CARRY_ROWS, POOL_CARRY_ROWS + tm)
    zbuf_ref[body, :] = zp
    s2 = zp + _shifted(zbuf_ref, 1, tm, slice(0, pool_w))
    s2buf_ref[body, :] = s2
    s4 = s2 + _shifted(s2buf_ref, 2, tm, slice(0, pool_w))
    s4buf_ref[body, :] = s4[:, hi]
    s8 = s4[:, hi] + _shifted(s4buf_ref, 4, tm, lo)
    s8buf_ref[body, :] = s8
    s16 = s8 + _shifted(s8buf_ref, 8, tm, lo)
    for buf in bufs:
        buf[0:POOL_CARRY_ROWS, :] = buf[tm:tm + POOL_CARRY_ROWS, :]
    lane = lax.broadcasted_iota(jnp.int32, (tm, LANES), 1)
    first = lane < POOL_GROUP_DIM
    t1 = (pl.program_id(1) * tm + 1
          + lax.broadcasted_iota(jnp.int32, (tm, 1), 0)).astype(F32)
    inv = [1.0 / jnp.minimum(t1, float(w)) for w in POOL_WINDOWS]
    mean_lo = jnp.where(first, s2[:, lo] * inv[0], s4[:, lo] * inv[1])
    mean_hi = jnp.where(first, s8 * inv[2], s16 * inv[3])
    pooled = (jnp.concatenate([mean_lo, mean_hi], axis=1) - zp).astype(BF16)
    c_ref[...] = (_dot(pooled, pool_w_ref[...]) * pool_scale_ref[...]).astype(BF16)

    half = QK_ROPE // 2
    cpad = jnp.pad(cos_ref[...], ((0, 0), (0, LANES - half)))
    spad = jnp.pad(sin_ref[...], ((0, 0), (0, LANES - half)))
    cosv = cpad + pltpu.roll(cpad, half, 1)
    sinv = spad + pltpu.roll(spad, half, 1)

    qn = _rms_norm(q_lat, qa_g_ref[...]).astype(BF16)
    q = _dot(qn, qb_ref[...])
    kvn = _rms_norm(kv_lat, kva_g_ref[...]).astype(BF16)
    kv = _dot(kvn, kvb_ref[...])
    scale = QK_DIM ** -0.5 * math.log2(math.e)
    nope_w = MLA_HEADS * QK_NOPE
    rope_w = MLA_HEADS * LANES
    q_gc, q_gs = qg_r_ref[...] * cosv, qg_sw_ref[...] * sinv
    kr_ss = jnp.sum(kr * kr, axis=-1, keepdims=True)
    kr_rot = kr * (kg_r_ref[...] * cosv) + kr_sw * (kg_sw_ref[...] * sinv)
    for hd in range(MLA_HEADS):
        q_n = q[:, hd * QK_NOPE:(hd + 1) * QK_NOPE]
        q_r = q[:, nope_w + hd * LANES:nope_w + (hd + 1) * LANES]
        q_sw = q[:, nope_w + rope_w + hd * LANES:nope_w + rope_w + (hd + 1) * LANES]
        ss = jnp.sum(q_n * q_n + q_r * q_r, axis=-1, keepdims=True)
        rinv = lax.rsqrt(ss * (1.0 / QK_DIM) + EPS) * scale
        q_ref[hd, :, 0:QK_NOPE] = (q_n * qg_n_ref[...] * rinv).astype(BF16)
        q_ref[hd, :, QK_NOPE:] = ((q_r * q_gc + q_sw * q_gs) * rinv).astype(BF16)

        k_n = kv[:, hd * (QK_NOPE + V_DIM):hd * (QK_NOPE + V_DIM) + QK_NOPE]
        ss = jnp.sum(k_n * k_n, axis=-1, keepdims=True) + kr_ss
        rinv = lax.rsqrt(ss * (1.0 / QK_DIM) + EPS)
        k_ref[hd, :, 0:QK_NOPE] = (k_n * kg_n_ref[...] * rinv).astype(BF16)
        k_ref[hd, :, QK_NOPE:] = (kr_rot * rinv).astype(BF16)
        v_ref[hd] = kv[:, hd * (QK_NOPE + V_DIM) + QK_NOPE:(hd + 1) * (QK_NOPE + V_DIM)].astype(BF16)


def _odd_pre(x, cos_t, sin_t, g, w_in, pool_w, pool_scale, qa_g, qb, kva_g, kvb,
             qg_n, qg_r, qg_sw, kg_n, kg_r, kg_sw, *, tm):
    b, s, d = x.shape
    pool_width = pool_w.shape[0]
    qk_pad = QK_NOPE + LANES
    tok = pl.BlockSpec((None, tm, d), lambda i, j: (i, j, 0))
    tab = pl.BlockSpec((None, tm, cos_t.shape[2]), lambda i, j: (i, j, 0))
    consts = [g, w_in, pool_w, pool_scale, qa_g, qb, kva_g, kvb, qg_n, qg_r, qg_sw, kg_n, kg_r, kg_sw]
    head_spec = lambda w: pl.BlockSpec((None, MLA_HEADS, tm, w), lambda i, j: (i, 0, j, 0))
    carry = lambda w: pltpu.VMEM((tm + POOL_CARRY_ROWS, w), F32)
    return pl.pallas_call(
        functools.partial(_odd_pre_kernel, tm=tm),
        grid=(b, s // tm),
        in_specs=[tok, tab, tab] + [_full_spec(c) for c in consts],
        out_specs=[pl.BlockSpec((None, tm, pool_width), lambda i, j: (i, j, 0)),
                   head_spec(qk_pad), head_spec(qk_pad), head_spec(V_DIM)],
        out_shape=[jax.ShapeDtypeStruct((b, s, pool_width), BF16),
                   jax.ShapeDtypeStruct((b, MLA_HEADS, s, qk_pad), BF16),
                   jax.ShapeDtypeStruct((b, MLA_HEADS, s, qk_pad), BF16),
                   jax.ShapeDtypeStruct((b, MLA_HEADS, s, V_DIM), BF16)],
        scratch_shapes=[carry(pool_width), carry(pool_width), carry(LANES), carry(LANES)],
        compiler_params=_params(),
        name="odd_pre",
    )(x, cos_t, sin_t, *consts)


def _attn_kernel(q_ref, k_ref, v_ref, o_ref, m_ref, l_ref, acc_ref, *, tq, hp):
    s_len = q_ref.shape[1]
    n_col = tq // LANES
    row = lax.broadcasted_iota(jnp.int32, (tq, tq), 0)
    col = lax.broadcasted_iota(jnp.int32, (tq, tq), 1)
    causal = row >= col

    def tile(hd, q0, k0, masked):
        q = q_ref[hd, pl.ds(q0, tq), :]
        k = k_ref[hd, pl.ds(k0, tq), :]
        v = v_ref[hd, pl.ds(k0, tq), :]
        sc = lax.dot_general(q, k, (((1,), (1,)), ((), ())), preferred_element_type=F32)
        if masked:
            sc = jnp.where(causal, sc, MASK_VALUE)
        cols = [sc[:, c * LANES:(c + 1) * LANES] for c in range(n_col)]
        cmax = functools.reduce(jnp.maximum, cols)
        m_prev = m_ref[hd]
        m_new = jnp.maximum(m_prev, jnp.max(cmax, axis=-1, keepdims=True))
        alpha = jnp.exp2(m_prev - m_new)
        ps = [jnp.exp2(c - m_new) for c in cols]
        l_ref[hd] = alpha * l_ref[hd] + functools.reduce(jnp.add, ps)
        p = jnp.concatenate(ps, axis=1).astype(BF16)
        acc_ref[hd] = alpha * acc_ref[hd] + _dot(p, v)
        m_ref[hd] = m_new

    def q_body(qi, _):
        q0 = pl.multiple_of(qi * tq, tq)
        m_ref[...] = jnp.full(m_ref.shape, MASK_VALUE, F32)
        l_ref[...] = jnp.zeros(l_ref.shape, F32)
        acc_ref[...] = jnp.zeros(acc_ref.shape, F32)

        def k_body(kj, _):
            k0 = pl.multiple_of(kj * tq, tq)
            for hd in range(hp):
                tile(hd, q0, k0, False)
            return 0

        lax.fori_loop(0, qi, k_body, 0)
        for hd in range(hp):
            tile(hd, q0, q0, True)
        for hd in range(hp):
            l = jnp.sum(l_ref[hd], axis=-1, keepdims=True)
            o_ref[pl.ds(q0, tq), hd * V_DIM:(hd + 1) * V_DIM] = (acc_ref[hd] / l).astype(BF16)
        return 0

    lax.fori_loop(0, s_len // tq, q_body, 0)


def _attention(q, k, v, *, tq, hp):
    b, nh, s, qk_pad = q.shape
    qk_spec = pl.BlockSpec((None, hp, s, qk_pad), lambda i, j: (i, j, 0, 0))
    return pl.pallas_call(
        functools.partial(_attn_kernel, tq=tq, hp=hp),
        grid=(b, nh // hp),
        in_specs=[qk_spec, qk_spec,
                  pl.BlockSpec((None, hp, s, V_DIM), lambda i, j: (i, j, 0, 0))],
        out_specs=pl.BlockSpec((None, s, hp * V_DIM), lambda i, j: (i, 0, j)),
        out_shape=jax.ShapeDtypeStruct((b, s, nh * V_DIM), BF16),
        scratch_shapes=[pltpu.VMEM((hp, tq, LANES), F32), pltpu.VMEM((hp, tq, LANES), F32),
                        pltpu.VMEM((hp, tq, V_DIM), F32)],
        compiler_params=_params(),
        name="mla_attention",
    )(q, k, v)


def _pad_lanes(a, width):
    return jnp.pad(a, [(0, 0)] * (a.ndim - 1) + [(0, width - a.shape[-1])])


def _swap_halves(a, *, negate_first):
    a1, a2 = jnp.split(a, 2, axis=-1)
    return jnp.concatenate([-a2 if negate_first else a2, a1], axis=-1)


def kernel(x, positions, mix_norm, ffn_norm, even_w_in, sg_ln_g, sg_w_s, sg_b_s, sc_conv_w,
           even_w_out, odd_w_in, pool_w, pool_scale, q_a_norm, q_b, kv_a_norm, kv_b, q_norm,
           k_norm, odd_w_out, ffn_w_gate, ffn_w_up, ffn_w_down):
    depth = mix_norm.shape[0]
    tm = 512
    ff_chunk = 6 * MXU_WIDTH
    row = lambda a: a.reshape(1, -1)

    cos_t, sin_t = _rope_tables(positions, rows=1024)

    for layer in range(depth):
        i = layer // 2
        if layer % 2 == 0:
            mix = _even_mixer(
                x, row(mix_norm[layer]), even_w_in[i].astype(BF16), row(sg_ln_g[i]),
                sg_w_s[i], sg_b_s[i].T, sc_conv_w[i], tm=tm)
            mixes, w_out = [mix], even_w_out[i].astype(BF16)
        else:
            pool_width = pool_scale.shape[-1]
            n_groups = pool_w.shape[1]
            gd = pool_w.shape[2]
            w_bd = jnp.zeros((pool_width, pool_width), F32)
            for gidx in range(n_groups):
                w_bd = w_bd.at[gidx * gd:(gidx + 1) * gd, gidx * gd:(gidx + 1) * gd].set(pool_w[i, gidx])
            q_lora = q_b.shape[1]
            qb3 = q_b[i].reshape(q_lora, MLA_HEADS, QK_DIM)
            qb_nope = qb3[:, :, :QK_NOPE].reshape(q_lora, MLA_HEADS * QK_NOPE)
            qb_rope = _pad_lanes(qb3[:, :, QK_NOPE:], LANES).reshape(q_lora, MLA_HEADS * LANES)
            qb_sw = _pad_lanes(_swap_halves(qb3[:, :, QK_NOPE:], negate_first=True), LANES)
            qb = jnp.concatenate([qb_nope, qb_rope, qb_sw.reshape(q_lora, MLA_HEADS * LANES)],
                                 axis=1).astype(BF16)
            w_rope = odd_w_in[i][:, -QK_ROPE:]
            w_in = jnp.concatenate(
                [odd_w_in[i][:, :-QK_ROPE], _pad_lanes(w_rope, LANES),
                 _pad_lanes(_swap_halves(w_rope, negate_first=True), LANES)], axis=1).astype(BF16)
            gain_rows = []
            for gn in (q_norm[i], k_norm[i]):
                g_rope = row(gn[QK_NOPE:])
                gain_rows += [row(gn[:QK_NOPE]), _pad_lanes(g_rope, LANES),
                              _pad_lanes(_swap_halves(g_rope, negate_first=False), LANES)]
            c_out, q, k, v = _odd_pre(
                x, cos_t, sin_t, row(mix_norm[layer]), w_in, w_bd.astype(BF16), row(pool_scale[i]),
                row(q_a_norm[i]), qb, row(kv_a_norm[i]), kv_b[i].astype(BF16), *gain_rows, tm=tm)
            d_out = _attention(q, k, v, tq=512, hp=2)
            mixes, w_out = [c_out, d_out], odd_w_out[i].astype(BF16)
        x = _post(x, mixes, w_out, row(ffn_norm[layer]), ffn_w_gate[layer].astype(BF16),
                  ffn_w_up[layer].astype(BF16), ffn_w_down[layer].astype(BF16),
                  tm=tm, ff_chunk=ff_chunk)
    return x
```

```python
import functools
import math

import jax
import jax.numpy as jnp
from jax import lax
from jax.experimental import pallas as pl
from jax.experimental.pallas import tpu as pltpu

F32 = jnp.float32
BF16 = jnp.bfloat16

EPS = 1e-6
MASK_VALUE = -1e30
LANES = 128
MXU_WIDTH = 256
SG_CHUNK = 128
SG_HEADS = 4
CONV_WIDTH = 3
POOL_WINDOWS = (2, 4, 8, 16)
POOL_GROUP_DIM = 64
MLA_HEADS = 6
QK_NOPE = 128
QK_ROPE = 64
QK_DIM = QK_NOPE + QK_ROPE
V_DIM = 128
ROPE_THETA = 10000.0
CARRY_ROWS = 8
POOL_CARRY_ROWS = 16
VMEM_LIMIT_BYTES = 56 * 1024 * 1024


def _rms_norm(x, g):
    ms = jnp.mean(x * x, axis=-1, keepdims=True)
    return x * lax.rsqrt(ms + EPS) * g


def _gelu(x):
    return 0.5 * x * (1.0 + lax.erf(x * math.sqrt(0.5)))


def _dot(a, b):
    return jnp.dot(a, b, preferred_element_type=F32)


def _full_spec(arr):
    nd = arr.ndim
    return pl.BlockSpec(arr.shape, lambda *_: (0,) * nd)


def _params():
    return pltpu.CompilerParams(
        dimension_semantics=("arbitrary", "arbitrary"),
        vmem_limit_bytes=VMEM_LIMIT_BYTES)


def _even_mixer_kernel(x_ref, g_ref, w_in_ref, ln_g_ref, w_s_ref, b_st_ref,
                       conv_w_ref, mix_ref, zbuf_ref, *, tm):
    sg_w = SG_HEADS * LANES
    h = _rms_norm(x_ref[...], g_ref[...]).astype(BF16)
    proj = _dot(h, w_in_ref[...])
    u = _gelu(proj[:, 0:sg_w])
    v = _gelu(proj[:, sg_w:2 * sg_w])
    sc_w = (proj.shape[1] - 2 * sg_w) // 3
    b_gate = proj[:, 2 * sg_w:2 * sg_w + sc_w]
    c_gate = proj[:, 2 * sg_w + sc_w:2 * sg_w + 2 * sc_w]
    hv = proj[:, 2 * sg_w + 2 * sc_w:]

    row = lax.broadcasted_iota(jnp.int32, (SG_CHUNK, SG_CHUNK), 0)
    col = lax.broadcasted_iota(jnp.int32, (SG_CHUNK, SG_CHUNK), 1)
    causal = row >= col
    for hd in range(SG_HEADS):
        cs = slice(hd * LANES, (hd + 1) * LANES)
        vh = v[:, cs]
        mu = jnp.mean(vh, axis=-1, keepdims=True)
        xc = vh - mu
        var = jnp.mean(xc * xc, axis=-1, keepdims=True)
        vn = (xc * lax.rsqrt(var + EPS) * ln_g_ref[:, cs]).astype(BF16)
        w = jnp.where(causal, w_s_ref[hd], 0.0).astype(BF16)
        bias = b_st_ref[:, hd:hd + 1]
        for c in range(tm // SG_CHUNK):
            rs = slice(c * SG_CHUNK, (c + 1) * SG_CHUNK)
            mixed = _dot(w, vn[rs]) + bias
            mix_ref[rs, cs] = (u[rs, cs] * mixed).astype(BF16)

    @pl.when(pl.program_id(1) == 0)
    def _():
        zbuf_ref[0:CARRY_ROWS, :] = jnp.zeros((CARRY_ROWS, sc_w), F32)

    z = c_gate * hv
    zbuf_ref[CARRY_ROWS:CARRY_ROWS + tm, :] = z
    y = conv_w_ref[CONV_WIDTH - 1:CONV_WIDTH, :] * z
    for k in range(CONV_WIDTH - 1):
        shift = CONV_WIDTH - 1 - k
        y = y + conv_w_ref[k:k + 1, :] * zbuf_ref[CARRY_ROWS - shift:CARRY_ROWS - shift + tm, :]
    mix_ref[:, sg_w:] = (b_gate * y).astype(BF16)
    zbuf_ref[0:CARRY_ROWS, :] = zbuf_ref[tm:tm + CARRY_ROWS, :]


def _even_mixer(x, g, w_in, ln_g, w_s, b_st, conv_w, *, tm):
    b, s, d = x.shape
    sc_w = conv_w.shape[1]
    mix_w = SG_HEADS * LANES + sc_w
    tok = pl.BlockSpec((None, tm, d), lambda i, j: (i, j, 0))
    return pl.pallas_call(
        functools.partial(_even_mixer_kernel, tm=tm),
        grid=(b, s // tm),
        in_specs=[tok, _full_spec(g), _full_spec(w_in), _full_spec(ln_g),
                  _full_spec(w_s), _full_spec(b_st), _full_spec(conv_w)],
        out_specs=pl.BlockSpec((None, tm, mix_w), lambda i, j: (i, j, 0)),
        out_shape=jax.ShapeDtypeStruct((b, s, mix_w), BF16),
        scratch_shapes=[pltpu.VMEM((tm + CARRY_ROWS, sc_w), F32)],
        compiler_params=_params(),
        name="even_mixer",
    )(x, g, w_in, ln_g, w_s, b_st, conv_w)


def _post_kernel(*refs, n_mix, ff_chunk):
    x_ref = refs[0]
    mix_refs = refs[1:1 + n_mix]
    wo_ref, g_ref, wg_ref, wu_ref, wd_ref, o_ref = refs[1 + n_mix:]
    mix = [m_ref[...] for m_ref in mix_refs]
    mix = mix[0] if n_mix == 1 else jnp.concatenate(mix, axis=1)
    x1 = x_ref[...] + _dot(mix, wo_ref[...])
    h = _rms_norm(x1, g_ref[...]).astype(BF16)
    acc = x1
    d_ff = wg_ref.shape[1]
    for c0 in range(0, d_ff, ff_chunk):
        cs = slice(c0, min(c0 + ff_chunk, d_ff))
        gate = _dot(h, wg_ref[:, cs])
        up = _dot(h, wu_ref[:, cs])
        act = (gate / (1.0 + jnp.exp(-gate)) * up).astype(BF16)
        acc = acc + _dot(act, wd_ref[cs, :])
    o_ref[...] = acc


def _post(x, mixes, w_out, g, wg, wu, wd, *, tm, ff_chunk):
    b, s, d = x.shape
    tok = pl.BlockSpec((None, tm, d), lambda i, j: (i, j, 0))
    mix_specs = [pl.BlockSpec((None, tm, m.shape[2]), lambda i, j: (i, j, 0)) for m in mixes]
    return pl.pallas_call(
        functools.partial(_post_kernel, n_mix=len(mixes), ff_chunk=ff_chunk),
        grid=(b, s // tm),
        in_specs=[tok] + mix_specs
                 + [_full_spec(w_out), _full_spec(g), _full_spec(wg), _full_spec(wu), _full_spec(wd)],
        out_specs=tok,
        out_shape=jax.ShapeDtypeStruct(x.shape, x.dtype),
        compiler_params=_params(),
        name="outproj_ffn",
    )(x, *mixes, w_out, g, wg, wu, wd)


def _rope_tables(pos_ref, invf_ref):
    half = QK_ROPE // 2
    groups = LANES // half
    rows = pos_ref.shape[0]
    pos = pos_ref[...].astype(F32)
    lane = lax.broadcasted_iota(jnp.int32, (rows, LANES), 1)
    p = pos[:, 0:1]
    for grp in range(1, groups):
        p = jnp.where(lane >= grp * half, pos[:, grp:grp + 1], p)
    ang = p * invf_ref[...]
    tables = []
    for packed in (jnp.cos(ang), jnp.sin(ang)):
        quarters = []
        for grp in range(groups):
            r = packed if grp == 0 else pltpu.roll(packed, LANES - grp * half, 1)
            quarters.append(jnp.where(lane < half, r, pltpu.roll(r, half, 1)))
        tables.append(jnp.concatenate(quarters, axis=0))
    return tables


def _shifted(buf_ref, shift, tm, cs):
    return buf_ref[POOL_CARRY_ROWS - shift:POOL_CARRY_ROWS - shift + tm, cs]


def _odd_pre_kernel(x_ref, pos_ref, invf_ref, g_ref, w_in_ref, pool_w_ref, pool_scale_ref,
                    qa_g_ref, qb_ref, kva_g_ref, kvb_ref, qg_n_ref, qg_r_ref, qg_sw_ref,
                    kg_n_ref, kg_r_ref, kg_sw_ref,
                    c_ref, q_ref, k_ref, v_ref, zbuf_ref, s2buf_ref, s4buf_ref, s8buf_ref, *, tm):
    pool_w = len(POOL_WINDOWS) * POOL_GROUP_DIM
    q_lora = qa_g_ref.shape[1]
    kv_lora = kva_g_ref.shape[1]
    h = _rms_norm(x_ref[...], g_ref[...]).astype(BF16)
    proj = _dot(h, w_in_ref[...])
    zp = proj[:, 0:pool_w]
    q_lat = proj[:, pool_w:pool_w + q_lora]
    kv_lat = proj[:, pool_w + q_lora:pool_w + q_lora + kv_lora]
    kr_off = pool_w + q_lora + kv_lora
    kr = proj[:, kr_off:kr_off + LANES]
    kr_sw = proj[:, kr_off + LANES:kr_off + 2 * LANES]

    lo, hi = slice(0, LANES), slice(LANES, 2 * LANES)
    bufs = (zbuf_ref, s2buf_ref, s4buf_ref, s8buf_ref)

    @pl.when(pl.program_id(1) == 0)
    def _():
        for buf in bufs:
            buf[0:POOL_CARRY_ROWS, :] = jnp.zeros((POOL_CARRY_ROWS, buf.shape[1]), F32)

    body = slice(POOL_CARRY_ROWS, POOL_CARRY_ROWS + tm)
    zbuf_ref[body, :] = zp
    s2 = zp + _shifted(zbuf_ref, 1, tm, slice(0, pool_w))
    s2buf_ref[body, :] = s2
    s4 = s2 + _shifted(s2buf_ref, 2, tm, slice(0, pool_w))
    s4buf_ref[body, :] = s4[:, hi]
    s8 = s4[:, hi] + _shifted(s4buf_ref, 4, tm, lo)
    s8buf_ref[body, :] = s8
    s16 = s8 + _shifted(s8buf_ref, 8, tm, lo)
    for buf in bufs:
        buf[0:POOL_CARRY_ROWS, :] = buf[tm:tm + POOL_CARRY_ROWS, :]
    lane = lax.broadcasted_iota(jnp.int32, (tm, LANES), 1)
    first = lane < POOL_GROUP_DIM
    t1 = (pl.program_id(1) * tm + 1
          + lax.broadcasted_iota(jnp.int32, (tm, 1), 0)).astype(F32)
    inv = [1.0 / jnp.minimum(t1, float(w)) for w in POOL_WINDOWS]
    mean_lo = jnp.where(first, s2[:, lo] * inv[0], s4[:, lo] * inv[1])
    mean_hi = jnp.where(first, s8 * inv[2], s16 * inv[3])
    pooled = (jnp.concatenate([mean_lo, mean_hi], axis=1) - zp).astype(BF16)
    c_ref[...] = (_dot(pooled, pool_w_ref[...]) * pool_scale_ref[...]).astype(BF16)

    cosv, sinv = _rope_tables(pos_ref, invf_ref)

    qn = _rms_norm(q_lat, qa_g_ref[...]).astype(BF16)
    q = _dot(qn, qb_ref[...])
    kvn = _rms_norm(kv_lat, kva_g_ref[...]).astype(BF16)
    kv = _dot(kvn, kvb_ref[...])
    scale = QK_DIM ** -0.5 * math.log2(math.e)
    nope_w = MLA_HEADS * QK_NOPE
    rope_w = MLA_HEADS * LANES
    q_gc, q_gs = qg_r_ref[...] * cosv, qg_sw_ref[...] * sinv
    kr_ss = jnp.sum(kr * kr, axis=-1, keepdims=True)
    kr_rot = kr * (kg_r_ref[...] * cosv) + kr_sw * (kg_sw_ref[...] * sinv)
    for hd in range(MLA_HEADS):
        q_n = q[:, hd * QK_NOPE:(hd + 1) * QK_NOPE]
        q_r = q[:, nope_w + hd * LANES:nope_w + (hd + 1) * LANES]
        q_sw = q[:, nope_w + rope_w + hd * LANES:nope_w + rope_w + (hd + 1) * LANES]
        ss = jnp.sum(q_n * q_n + q_r * q_r, axis=-1, keepdims=True)
        rinv = lax.rsqrt(ss * (1.0 / QK_DIM) + EPS) * scale
        q_ref[hd, :, 0:QK_NOPE] = (q_n * qg_n_ref[...] * rinv).astype(BF16)
        q_ref[hd, :, QK_NOPE:] = ((q_r * q_gc + q_sw * q_gs) * rinv).astype(BF16)

        k_n = kv[:, hd * (QK_NOPE + V_DIM):hd * (QK_NOPE + V_DIM) + QK_NOPE]
        ss = jnp.sum(k_n * k_n, axis=-1, keepdims=True) + kr_ss
        rinv = lax.rsqrt(ss * (1.0 / QK_DIM) + EPS)
        k_ref[hd, :, 0:QK_NOPE] = (k_n * kg_n_ref[...] * rinv).astype(BF16)
        k_ref[hd, :, QK_NOPE:] = (kr_rot * rinv).astype(BF16)
        v_ref[hd] = kv[:, hd * (QK_NOPE + V_DIM) + QK_NOPE:(hd + 1) * (QK_NOPE + V_DIM)].astype(BF16)


def _odd_pre(x, positions, g, w_in, pool_w, pool_scale, qa_g, qb, kva_g, kvb,
             qg_n, qg_r, qg_sw, kg_n, kg_r, kg_sw, *, tm):
    b, s, d = x.shape
    pool_width = pool_w.shape[0]
    qk_pad = QK_NOPE + LANES
    groups = LANES // (QK_ROPE // 2)
    pos = positions.reshape(b, s // tm, groups, tm // groups).transpose(0, 1, 3, 2)
    inv_freq = ROPE_THETA ** (-jnp.arange(0, QK_ROPE, 2, dtype=F32) / QK_ROPE)
    invf = jnp.tile(inv_freq, groups).reshape(1, LANES)
    tok = pl.BlockSpec((None, tm, d), lambda i, j: (i, j, 0))
    pos_spec = pl.BlockSpec((None, None, tm // groups, groups), lambda i, j: (i, j, 0, 0))
    consts = [invf, g, w_in, pool_w, pool_scale, qa_g, qb, kva_g, kvb, qg_n, qg_r, qg_sw, kg_n, kg_r, kg_sw]
    head_spec = lambda w: pl.BlockSpec((None, MLA_HEADS, tm, w), lambda i, j: (i, 0, j, 0))
    carry = lambda w: pltpu.VMEM((tm + POOL_CARRY_ROWS, w), F32)
    return pl.pallas_call(
        functools.partial(_odd_pre_kernel, tm=tm),
        grid=(b, s // tm),
        in_specs=[tok, pos_spec] + [_full_spec(c) for c in consts],
        out_specs=[pl.BlockSpec((None, tm, pool_width), lambda i, j: (i, j, 0)),
                   head_spec(qk_pad), head_spec(qk_pad), head_spec(V_DIM)],
        out_shape=[jax.ShapeDtypeStruct((b, s, pool_width), BF16),
                   jax.ShapeDtypeStruct((b, MLA_HEADS, s, qk_pad), BF16),
                   jax.ShapeDtypeStruct((b, MLA_HEADS, s, qk_pad), BF16),
                   jax.ShapeDtypeStruct((b, MLA_HEADS, s, V_DIM), BF16)],
        scratch_shapes=[carry(pool_width), carry(pool_width), carry(LANES), carry(LANES)],
        compiler_params=_params(),
        name="odd_pre",
    )(x, pos, *consts)


def _attn_kernel(q_ref, k_ref, v_ref, o_ref, *, tq, hp):
    s_len = q_ref.shape[1]
    n_col = tq // LANES
    row = lax.broadcasted_iota(jnp.int32, (tq, tq), 0)
    col = lax.broadcasted_iota(jnp.int32, (tq, tq), 1)
    causal = row >= col

    def tile(hd, q0, k0, state):
        q = q_ref[hd, q0:q0 + tq, :]
        k = k_ref[hd, k0:k0 + tq, :]
        v = v_ref[hd, k0:k0 + tq, :]
        sc = lax.dot_general(q, k, (((1,), (1,)), ((), ())), preferred_element_type=F32)
        if k0 == q0:
            sc = jnp.where(causal, sc, MASK_VALUE)
        cols = [sc[:, c * LANES:(c + 1) * LANES] for c in range(n_col)]
        rmax = jnp.max(functools.reduce(jnp.maximum, cols), axis=-1, keepdims=True)
        if state is None:
            m_new = jnp.broadcast_to(rmax, (tq, LANES))
            ps = [jnp.exp2(c - m_new) for c in cols]
            return m_new, functools.reduce(jnp.add, ps), _dot(_bf16_cat(ps), v)
        m_prev, l_prev, acc_prev = state
        m_new = jnp.maximum(m_prev, rmax)
        alpha = jnp.exp2(m_prev - m_new)
        ps = [jnp.exp2(c - m_new) for c in cols]
        return (m_new, alpha * l_prev + functools.reduce(jnp.add, ps),
                alpha * acc_prev + _dot(_bf16_cat(ps), v))

    for qi in range(s_len // tq):
        for hd in range(hp):
            state = None
            for kj in range(qi + 1):
                state = tile(hd, qi * tq, kj * tq, state)
            _, l_part, acc = state
            l = jnp.sum(l_part, axis=-1, keepdims=True)
            o_ref[qi * tq:(qi + 1) * tq, hd * V_DIM:(hd + 1) * V_DIM] = (acc / l).astype(BF16)


def _bf16_cat(cols):
    return jnp.concatenate(cols, axis=1).astype(BF16)


def _attention(q, k, v, *, tq, hp):
    b, nh, s, qk_pad = q.shape
    qk_spec = pl.BlockSpec((None, hp, s, qk_pad), lambda i, j: (i, j, 0, 0))
    return pl.pallas_call(
        functools.partial(_attn_kernel, tq=tq, hp=hp),
        grid=(b, nh // hp),
        in_specs=[qk_spec, qk_spec,
                  pl.BlockSpec((None, hp, s, V_DIM), lambda i, j: (i, j, 0, 0))],
        out_specs=pl.BlockSpec((None, s, hp * V_DIM), lambda i, j: (i, 0, j)),
        out_shape=jax.ShapeDtypeStruct((b, s, nh * V_DIM), BF16),
        compiler_params=_params(),
        name="mla_attention",
    )(q, k, v)


def _pad_lanes(a, width):
    return jnp.pad(a, [(0, 0)] * (a.ndim - 1) + [(0, width - a.shape[-1])])


def _swap_halves(a, *, negate_first):
    a1, a2 = jnp.split(a, 2, axis=-1)
    return jnp.concatenate([-a2 if negate_first else a2, a1], axis=-1)


def kernel(x, positions, mix_norm, ffn_norm, even_w_in, sg_ln_g, sg_w_s, sg_b_s, sc_conv_w,
           even_w_out, odd_w_in, pool_w, pool_scale, q_a_norm, q_b, kv_a_norm, kv_b, q_norm,
           k_norm, odd_w_out, ffn_w_gate, ffn_w_up, ffn_w_down):
    depth = mix_norm.shape[0]
    tm = 512
    ff_chunk = 6 * MXU_WIDTH
    row = lambda a: a.reshape(1, -1)

    for layer in range(depth):
        i = layer // 2
        if layer % 2 == 0:
            mix = _even_mixer(
                x, row(mix_norm[layer]), even_w_in[i].astype(BF16), row(sg_ln_g[i]),
                sg_w_s[i], sg_b_s[i].T, sc_conv_w[i], tm=tm)
            mixes, w_out = [mix], even_w_out[i].astype(BF16)
        else:
            pool_width = pool_scale.shape[-1]
            n_groups = pool_w.shape[1]
            gd = pool_w.shape[2]
            w_bd = jnp.zeros((pool_width, pool_width), F32)
            for gidx in range(n_groups):
                w_bd = w_bd.at[gidx * gd:(gidx + 1) * gd, gidx * gd:(gidx + 1) * gd].set(pool_w[i, gidx])
            q_lora = q_b.shape[1]
            qb3 = q_b[i].reshape(q_lora, MLA_HEADS, QK_DIM)
            qb_nope = qb3[:, :, :QK_NOPE].reshape(q_lora, MLA_HEADS * QK_NOPE)
            qb_rope = _pad_lanes(qb3[:, :, QK_NOPE:], LANES).reshape(q_lora, MLA_HEADS * LANES)
            qb_sw = _pad_lanes(_swap_halves(qb3[:, :, QK_NOPE:], negate_first=True), LANES)
            qb = jnp.concatenate([qb_nope, qb_rope, qb_sw.reshape(q_lora, MLA_HEADS * LANES)],
                                 axis=1).astype(BF16)
            w_rope = odd_w_in[i][:, -QK_ROPE:]
            w_in = jnp.concatenate(
                [odd_w_in[i][:, :-QK_ROPE], _pad_lanes(w_rope, LANES),
                 _pad_lanes(_swap_halves(w_rope, negate_first=True), LANES)], axis=1).astype(BF16)
            gain_rows = []
            for gn in (q_norm[i], k_norm[i]):
                g_rope = row(gn[QK_NOPE:])
                gain_rows += [row(gn[:QK_NOPE]), _pad_lanes(g_rope, LANES),
                              _pad_lanes(_swap_halves(g_rope, negate_first=False), LANES)]
            c_out, q, k, v = _odd_pre(
                x, positions, row(mix_norm[layer]), w_in, w_bd.astype(BF16), row(pool_scale[i]),
                row(q_a_norm[i]), qb, row(kv_a_norm[i]), kv_b[i].astype(BF16), *gain_rows, tm=tm)
            d_out = _attention(q, k, v, tq=512, hp=2)
            mixes, w_out = [c_out, d_out], odd_w_out[i].astype(BF16)
        x = _post(x, mixes, w_out, row(ffn_norm[layer]), ffn_w_gate[layer].astype(BF16),
                  ffn_w_up[layer].astype(BF16), ffn_w_down[layer].astype(BF16),
                  tm=tm, ff_chunk=ff_chunk)
    return x
```

```python
import functools
import math

import jax
import jax.numpy as jnp
from jax import lax
from jax.experimental import pallas as pl
from jax.experimental.pallas import tpu as pltpu

F32 = jnp.float32
BF16 = jnp.bfloat16

EPS = 1e-6
MASK_VALUE = -1e30
LANES = 128
MXU_WIDTH = 256
SG_CHUNK = 128
SG_HEADS = 4
CONV_WIDTH = 3
POOL_WINDOWS = (2, 4, 8, 16)
POOL_GROUP_DIM = 64
MLA_HEADS = 6
QK_NOPE = 128
QK_ROPE = 64
QK_DIM = QK_NOPE + QK_ROPE
V_DIM = 128
ROPE_THETA = 10000.0
CARRY_ROWS = 8
POOL_CARRY_ROWS = 16
VMEM_LIMIT_BYTES = 56 * 1024 * 1024


def _rms_norm(x, g):
    ms = jnp.mean(x * x, axis=-1, keepdims=True)
    return x * lax.rsqrt(ms + EPS) * g


def _gelu(x):
    return 0.5 * x * (1.0 + lax.erf(x * math.sqrt(0.5)))


def _dot(a, b):
    return jnp.dot(a, b, preferred_element_type=F32)


def _full_spec(arr):
    nd = arr.ndim
    return pl.BlockSpec(arr.shape, lambda *_: (0,) * nd)


def _params(grid_rank=2):
    return pltpu.CompilerParams(
        dimension_semantics=("arbitrary",) * grid_rank,
        vmem_limit_bytes=VMEM_LIMIT_BYTES)


def _even_mixer_kernel(x_ref, g_ref, w_in_ref, ln_g_ref, w_s_ref, b_st_ref,
                       conv_w_ref, mix_ref, zbuf_ref, *, tm):
    sg_w = SG_HEADS * LANES
    h = _rms_norm(x_ref[...], g_ref[...]).astype(BF16)
    proj = _dot(h, w_in_ref[...])
    u = _gelu(proj[:, 0:sg_w])
    v = _gelu(proj[:, sg_w:2 * sg_w])
    sc_w = (proj.shape[1] - 2 * sg_w) // 3
    b_gate = proj[:, 2 * sg_w:2 * sg_w + sc_w]
    c_gate = proj[:, 2 * sg_w + sc_w:2 * sg_w + 2 * sc_w]
    hv = proj[:, 2 * sg_w + 2 * sc_w:]

    row = lax.broadcasted_iota(jnp.int32, (SG_CHUNK, SG_CHUNK), 0)
    col = lax.broadcasted_iota(jnp.int32, (SG_CHUNK, SG_CHUNK), 1)
    causal = row >= col
    for hd in range(SG_HEADS):
        cs = slice(hd * LANES, (hd + 1) * LANES)
        vh = v[:, cs]
        mu = jnp.mean(vh, axis=-1, keepdims=True)
        xc = vh - mu
        var = jnp.mean(xc * xc, axis=-1, keepdims=True)
        vn = (xc * lax.rsqrt(var + EPS) * ln_g_ref[:, cs]).astype(BF16)
        w = jnp.where(causal, w_s_ref[hd], 0.0).astype(BF16)
        bias = b_st_ref[:, hd:hd + 1]
        for c in range(tm // SG_CHUNK):
            rs = slice(c * SG_CHUNK, (c + 1) * SG_CHUNK)
            mixed = _dot(w, vn[rs]) + bias
            mix_ref[rs, cs] = (u[rs, cs] * mixed).astype(BF16)

    @pl.when(pl.program_id(1) == 0)
    def _():
        zbuf_ref[0:CARRY_ROWS, :] = jnp.zeros((CARRY_ROWS, sc_w), F32)

    z = c_gate * hv
    zbuf_ref[CARRY_ROWS:CARRY_ROWS + tm, :] = z
    y = conv_w_ref[CONV_WIDTH - 1:CONV_WIDTH, :] * z
    for k in range(CONV_WIDTH - 1):
        shift = CONV_WIDTH - 1 - k
        y = y + conv_w_ref[k:k + 1, :] * zbuf_ref[CARRY_ROWS - shift:CARRY_ROWS - shift + tm, :]
    mix_ref[:, sg_w:] = (b_gate * y).astype(BF16)
    zbuf_ref[0:CARRY_ROWS, :] = zbuf_ref[tm:tm + CARRY_ROWS, :]


def _skewed_tile_maps(n_tiles, seq_tiles):
    def lead(t):
        tt = jnp.minimum(t, n_tiles - 1)
        return tt // seq_tiles, lax.rem(tt, seq_tiles)

    def trail(t):
        tt = jnp.maximum(t - 1, 0)
        return tt // seq_tiles, lax.rem(tt, seq_tiles)

    return lead, trail


def _even_mixer(x, g, w_in, ln_g, w_s, b_st, conv_w, *, tm):
    b, s, d = x.shape
    sc_w = conv_w.shape[1]
    mix_w = SG_HEADS * LANES + sc_w
    tok = pl.BlockSpec((None, tm, d), lambda i, j: (i, j, 0))
    return pl.pallas_call(
        functools.partial(_even_mixer_kernel, tm=tm),
        grid=(b, s // tm),
        in_specs=[tok, _full_spec(g), _full_spec(w_in), _full_spec(ln_g),
                  _full_spec(w_s), _full_spec(b_st), _full_spec(conv_w)],
        out_specs=pl.BlockSpec((None, tm, mix_w), lambda i, j: (i, j, 0)),
        out_shape=jax.ShapeDtypeStruct((b, s, mix_w), BF16),
        scratch_shapes=[pltpu.VMEM((tm + CARRY_ROWS, sc_w), F32)],
        compiler_params=_params(),
        name="even_mixer",
    )(x, g, w_in, ln_g, w_s, b_st, conv_w)


def _post_kernel(*refs, n_mix, ff_chunk):
    x_ref = refs[0]
    mix_refs = refs[1:1 + n_mix]
    wo_ref, g_ref, wg_ref, wu_ref, wd_ref, o_ref = refs[1 + n_mix:]
    mix = [m_ref[...] for m_ref in mix_refs]
    mix = mix[0] if n_mix == 1 else jnp.concatenate(mix, axis=1)
    x1 = x_ref[...] + _dot(mix, wo_ref[...])
    h = _rms_norm(x1, g_ref[...]).astype(BF16)
    acc = x1
    d_ff = wg_ref.shape[1]
    for c0 in range(0, d_ff, ff_chunk):
        cs = slice(c0, min(c0 + ff_chunk, d_ff))
        gate = _dot(h, wg_ref[:, cs])
        up = _dot(h, wu_ref[:, cs])
        act = (gate / (1.0 + jnp.exp(-gate)) * up).astype(BF16)
        acc = acc + _dot(act, wd_ref[cs, :])
    o_ref[...] = acc


def _post(x, mixes, w_out, g, wg, wu, wd, *, tm, ff_chunk):
    b, s, d = x.shape
    tok = pl.BlockSpec((None, tm, d), lambda i, j: (i, j, 0))
    mix_specs = [pl.BlockSpec((None, tm, m.shape[2]), lambda i, j: (i, j, 0)) for m in mixes]
    return pl.pallas_call(
        functools.partial(_post_kernel, n_mix=len(mixes), ff_chunk=ff_chunk),
        grid=(b, s // tm),
        in_specs=[tok] + mix_specs
                 + [_full_spec(w_out), _full_spec(g), _full_spec(wg), _full_spec(wu), _full_spec(wd)],
        out_specs=tok,
        out_shape=jax.ShapeDtypeStruct(x.shape, x.dtype),
        compiler_params=_params(),
        name="outproj_ffn",
    )(x, *mixes, w_out, g, wg, wu, wd)


def _rope_tables(pos_ref, invf_ref):
    half = QK_ROPE // 2
    groups = LANES // half
    rows = pos_ref.shape[0]
    pos = pos_ref[...].astype(F32)
    lane = lax.broadcasted_iota(jnp.int32, (rows, LANES), 1)
    p = pos[:, 0:1]
    for grp in range(1, groups):
        p = jnp.where(lane >= grp * half, pos[:, grp:grp + 1], p)
    ang = p * invf_ref[...]
    tables = []
    for packed in (jnp.cos(ang), jnp.sin(ang)):
        quarters = []
        for grp in range(groups):
            r = packed if grp == 0 else pltpu.roll(packed, LANES - grp * half, 1)
            quarters.append(jnp.where(lane < half, r, pltpu.roll(r, half, 1)))
        tables.append(jnp.concatenate(quarters, axis=0))
    return tables


def _shifted(buf_ref, shift, tm, cs):
    return buf_ref[POOL_CARRY_ROWS - shift:POOL_CARRY_ROWS - shift + tm, cs]


POOL_W = len(POOL_WINDOWS) * POOL_GROUP_DIM
ST_ZP = 0
ST_KR = ST_ZP + POOL_W
ST_KR_SW = ST_KR + LANES
ST_Q = ST_KR_SW + LANES
ST_KV = ST_Q + 3 * MLA_HEADS * LANES
ST_WIDTH = ST_KV + MLA_HEADS * (QK_NOPE + V_DIM)


def _odd_pre_matmuls(x_ref, g_ref, w_in_ref, qa_g_ref, qb_ref, kva_g_ref, kvb_ref, st_ref):
    q_lora = qa_g_ref.shape[1]
    kv_lora = kva_g_ref.shape[1]
    h = _rms_norm(x_ref[...], g_ref[...]).astype(BF16)
    proj = _dot(h, w_in_ref[...])
    q_lat = proj[:, POOL_W:POOL_W + q_lora]
    kv_lat = proj[:, POOL_W + q_lora:POOL_W + q_lora + kv_lora]
    kr_off = POOL_W + q_lora + kv_lora
    st_ref[:, ST_ZP:ST_KR] = proj[:, 0:POOL_W]
    st_ref[:, ST_KR:ST_Q] = proj[:, kr_off:kr_off + 2 * LANES]
    qn = _rms_norm(q_lat, qa_g_ref[...]).astype(BF16)
    st_ref[:, ST_Q:ST_KV] = _dot(qn, qb_ref[...])
    kvn = _rms_norm(kv_lat, kva_g_ref[...]).astype(BF16)
    st_ref[:, ST_KV:ST_WIDTH] = _dot(kvn, kvb_ref[...])


def _odd_pre_tail(st_ref, pos_ref, invf_ref, pool_w_ref, pool_scale_ref, qg_n_ref, qg_r_ref,
                  qg_sw_ref, kg_n_ref, kg_r_ref, kg_sw_ref, c_ref, q_ref, k_ref, v_ref,
                  bufs, seq_tile, tm):
    pool_w = POOL_W
    zbuf_ref, s2buf_ref, s4buf_ref, s8buf_ref = bufs
    zp = st_ref[:, ST_ZP:ST_KR]
    kr = st_ref[:, ST_KR:ST_KR_SW]
    kr_sw = st_ref[:, ST_KR_SW:ST_Q]

    lo, hi = slice(0, LANES), slice(LANES, 2 * LANES)

    for buf in bufs:
        buf[0:POOL_CARRY_ROWS, :] = jnp.where(seq_tile == 0, 0.0, buf[0:POOL_CARRY_ROWS, :])

    body = slice(POOL_CARRY_ROWS, POOL_CARRY_ROWS + tm)
    zbuf_ref[body, :] = zp
    s2 = zp + _shifted(zbuf_ref, 1, tm, slice(0, pool_w))
    s2buf_ref[body, :] = s2
    s4 = s2 + _shifted(s2buf_ref, 2, tm, slice(0, pool_w))
    s4buf_ref[body, :] = s4[:, hi]
    s8 = s4[:, hi] + _shifted(s4buf_ref, 4, tm, lo)
    s8buf_ref[body, :] = s8
    s16 = s8 + _shifted(s8buf_ref, 8, tm, lo)
    for buf in bufs:
        buf[0:POOL_CARRY_ROWS, :] = buf[tm:tm + POOL_CARRY_ROWS, :]
    lane = lax.broadcasted_iota(jnp.int32, (tm, LANES), 1)
    first = lane < POOL_GROUP_DIM
    t1 = (seq_tile * tm + 1
          + lax.broadcasted_iota(jnp.int32, (tm, 1), 0)).astype(F32)
    inv = [1.0 / jnp.minimum(t1, float(w)) for w in POOL_WINDOWS]
    mean_lo = jnp.where(first, s2[:, lo] * inv[0], s4[:, lo] * inv[1])
    mean_hi = jnp.where(first, s8 * inv[2], s16 * inv[3])
    pooled = (jnp.concatenate([mean_lo, mean_hi], axis=1) - zp).astype(BF16)
    c_ref[...] = (_dot(pooled, pool_w_ref[...]) * pool_scale_ref[...]).astype(BF16)

    cosv, sinv = _rope_tables(pos_ref, invf_ref)

    scale = QK_DIM ** -0.5 * math.log2(math.e)
    rope_w = MLA_HEADS * LANES
    q_gc, q_gs = qg_r_ref[...] * cosv, qg_sw_ref[...] * sinv
    kr_ss = jnp.sum(kr * kr, axis=-1, keepdims=True)
    kr_rot = kr * (kg_r_ref[...] * cosv) + kr_sw * (kg_sw_ref[...] * sinv)
    for hd in range(MLA_HEADS):
        q0 = ST_Q + hd * LANES
        q_n = st_ref[:, q0:q0 + LANES]
        q_r = st_ref[:, q0 + rope_w:q0 + rope_w + LANES]
        q_sw = st_ref[:, q0 + 2 * rope_w:q0 + 2 * rope_w + LANES]
        ss = jnp.sum(q_n * q_n + q_r * q_r, axis=-1, keepdims=True)
        rinv = lax.rsqrt(ss * (1.0 / QK_DIM) + EPS) * scale
        q_ref[hd, :, 0:QK_NOPE] = (q_n * qg_n_ref[...] * rinv).astype(BF16)
        q_ref[hd, :, QK_NOPE:] = ((q_r * q_gc + q_sw * q_gs) * rinv).astype(BF16)

        k0 = ST_KV + hd * (QK_NOPE + V_DIM)
        k_n = st_ref[:, k0:k0 + QK_NOPE]
        ss = jnp.sum(k_n * k_n, axis=-1, keepdims=True) + kr_ss
        rinv = lax.rsqrt(ss * (1.0 / QK_DIM) + EPS)
        k_ref[hd, :, 0:QK_NOPE] = (k_n * kg_n_ref[...] * rinv).astype(BF16)
        k_ref[hd, :, QK_NOPE:] = (kr_rot * rinv).astype(BF16)
        v_ref[hd] = st_ref[:, k0 + QK_NOPE:k0 + QK_NOPE + V_DIM].astype(BF16)


def _odd_pre_kernel(x_ref, pos_ref, invf_ref, g_ref, w_in_ref, pool_w_ref, pool_scale_ref,
                    qa_g_ref, qb_ref, kva_g_ref, kvb_ref, qg_n_ref, qg_r_ref, qg_sw_ref,
                    kg_n_ref, kg_r_ref, kg_sw_ref,
                    c_ref, q_ref, k_ref, v_ref, st_ref, zbuf_ref, s2buf_ref, s4buf_ref, s8buf_ref,
                    *, tm, seq_tiles):
    t = pl.program_id(0)
    seq_tile = lax.rem(jnp.maximum(t - 1, 0), seq_tiles)
    bufs = (zbuf_ref, s2buf_ref, s4buf_ref, s8buf_ref)

    @pl.when(t == 0)
    def _():
        st_ref[1] = jnp.zeros(st_ref.shape[1:], F32)
        for buf in bufs:
            buf[0:POOL_CARRY_ROWS, :] = jnp.zeros((POOL_CARRY_ROWS, buf.shape[1]), F32)

    def step(slot):
        _odd_pre_matmuls(x_ref, g_ref, w_in_ref, qa_g_ref, qb_ref, kva_g_ref, kvb_ref,
                         st_ref.at[slot])
        _odd_pre_tail(st_ref.at[1 - slot], pos_ref, invf_ref, pool_w_ref, pool_scale_ref,
                      qg_n_ref, qg_r_ref, qg_sw_ref, kg_n_ref, kg_r_ref, kg_sw_ref,
                      c_ref, q_ref, k_ref, v_ref, bufs, seq_tile, tm)

    for slot in range(2):
        pl.when(lax.rem(t, 2) == slot)(functools.partial(step, slot))


def _odd_pre(x, positions, g, w_in, pool_w, pool_scale, qa_g, qb, kva_g, kvb,
             qg_n, qg_r, qg_sw, kg_n, kg_r, kg_sw, *, tm):
    b, s, d = x.shape
    pool_width = pool_w.shape[0]
    qk_pad = QK_NOPE + LANES
    groups = LANES // (QK_ROPE // 2)
    seq_tiles = s // tm
    n_tiles = b * seq_tiles
    pos = positions.reshape(b, seq_tiles, groups, tm // groups).transpose(0, 1, 3, 2)
    inv_freq = ROPE_THETA ** (-jnp.arange(0, QK_ROPE, 2, dtype=F32) / QK_ROPE)
    invf = jnp.tile(inv_freq, groups).reshape(1, LANES)

    lead, trail = _skewed_tile_maps(n_tiles, seq_tiles)

    def head_map(t):
        i, j = trail(t)
        return i, 0, j, 0

    consts = [invf, g, w_in, pool_w, pool_scale, qa_g, qb, kva_g, kvb,
              qg_n, qg_r, qg_sw, kg_n, kg_r, kg_sw]
    head_spec = lambda w: pl.BlockSpec((None, MLA_HEADS, tm, w), head_map)
    carry = lambda w: pltpu.VMEM((tm + POOL_CARRY_ROWS, w), F32)
    return pl.pallas_call(
        functools.partial(_odd_pre_kernel, tm=tm, seq_tiles=seq_tiles),
        grid=(n_tiles + 1,),
        in_specs=[pl.BlockSpec((None, tm, d), lambda t: (*lead(t), 0)),
                  pl.BlockSpec((None, None, tm // groups, groups), lambda t: (*trail(t), 0, 0))]
                 + [_full_spec(c) for c in consts],
        out_specs=[pl.BlockSpec((None, tm, pool_width), lambda t: (*trail(t), 0)),
                   head_spec(qk_pad), head_spec(qk_pad), head_spec(V_DIM)],
        out_shape=[jax.ShapeDtypeStruct((b, s, pool_width), BF16),
                   jax.ShapeDtypeStruct((b, MLA_HEADS, s, qk_pad), BF16),
                   jax.ShapeDtypeStruct((b, MLA_HEADS, s, qk_pad), BF16),
                   jax.ShapeDtypeStruct((b, MLA_HEADS, s, V_DIM), BF16)],
        scratch_shapes=[pltpu.VMEM((2, tm, ST_WIDTH), F32),
                        carry(pool_width), carry(pool_width), carry(LANES), carry(LANES)],
        compiler_params=_params(1),
        name="odd_pre",
    )(x, pos, *consts)


def _attn_kernel(q_ref, k_ref, v_ref, o_ref, *, tq, hp):
    s_len = q_ref.shape[1]
    n_col = tq // LANES
    row = lax.broadcasted_iota(jnp.int32, (tq, tq), 0)
    col = lax.broadcasted_iota(jnp.int32, (tq, tq), 1)
    causal = row >= col

    def tile(hd, q0, k0, state):
        q = q_ref[hd, q0:q0 + tq, :]
        k = k_ref[hd, k0:k0 + tq, :]
        v = v_ref[hd, k0:k0 + tq, :]
        sc = lax.dot_general(q, k, (((1,), (1,)), ((), ())), preferred_element_type=F32)
        if k0 == q0:
            sc = jnp.where(causal, sc, MASK_VALUE)
        cols = [sc[:, c * LANES:(c + 1) * LANES] for c in range(n_col)]
        rmax = jnp.max(functools.reduce(jnp.maximum, cols), axis=-1, keepdims=True)
        if state is None:
            m_new = jnp.broadcast_to(rmax, (tq, LANES))
            ps = [jnp.exp2(c - m_new) for c in cols]
            return m_new, functools.reduce(jnp.add, ps), _dot(_bf16_cat(ps), v)
        m_prev, l_prev, acc_prev = state
        m_new = jnp.maximum(m_prev, rmax)
        alpha = jnp.exp2(m_prev - m_new)
        ps = [jnp.exp2(c - m_new) for c in cols]
        return (m_new, alpha * l_prev + functools.reduce(jnp.add, ps),
                alpha * acc_prev + _dot(_bf16_cat(ps), v))

    for qi in range(s_len // tq):
        for hd in range(hp):
            state = None
            for kj in range(qi + 1):
                state = tile(hd, qi * tq, kj * tq, state)
            _, l_part, acc = state
            l = jnp.sum(l_part, axis=-1, keepdims=True)
            o_ref[qi * tq:(qi + 1) * tq, hd * V_DIM:(hd + 1) * V_DIM] = (acc / l).astype(BF16)


def _bf16_cat(cols):
    return jnp.concatenate(cols, axis=1).astype(BF16)


def _attention(q, k, v, *, tq, hp):
    b, nh, s, qk_pad = q.shape
    qk_spec = pl.BlockSpec((None, hp, s, qk_pad), lambda i, j: (i, j, 0, 0))
    return pl.pallas_call(
        functools.partial(_attn_kernel, tq=tq, hp=hp),
        grid=(b, nh // hp),
        in_specs=[qk_spec, qk_spec,
                  pl.BlockSpec((None, hp, s, V_DIM), lambda i, j: (i, j, 0, 0))],
        out_specs=pl.BlockSpec((None, s, hp * V_DIM), lambda i, j: (i, 0, j)),
        out_shape=jax.ShapeDtypeStruct((b, s, nh * V_DIM), BF16),
        compiler_params=_params(),
        name="mla_attention",
    )(q, k, v)


def _pad_lanes(a, width):
    return jnp.pad(a, [(0, 0)] * (a.ndim - 1) + [(0, width - a.shape[-1])])


def _swap_halves(a, *, negate_first):
    a1, a2 = jnp.split(a, 2, axis=-1)
    return jnp.concatenate([-a2 if negate_first else a2, a1], axis=-1)


def kernel(x, positions, mix_norm, ffn_norm, even_w_in, sg_ln_g, sg_w_s, sg_b_s, sc_conv_w,
           even_w_out, odd_w_in, pool_w, pool_scale, q_a_norm, q_b, kv_a_norm, kv_b, q_norm,
           k_norm, odd_w_out, ffn_w_gate, ffn_w_up, ffn_w_down):
    depth = mix_norm.shape[0]
    tm = 512
    ff_chunk = 6 * MXU_WIDTH
    row = lambda a: a.reshape(1, -1)

    for layer in range(depth):
        i = layer // 2
        if layer % 2 == 0:
            mix = _even_mixer(
                x, row(mix_norm[layer]), even_w_in[i].astype(BF16), row(sg_ln_g[i]),
                sg_w_s[i], sg_b_s[i].T, sc_conv_w[i], tm=tm)
            mixes, w_out = [mix], even_w_out[i].astype(BF16)
        else:
            pool_width = pool_scale.shape[-1]
            n_groups = pool_w.shape[1]
            gd = pool_w.shape[2]
            w_bd = jnp.zeros((pool_width, pool_width), F32)
            for gidx in range(n_groups):
                w_bd = w_bd.at[gidx * gd:(gidx + 1) * gd, gidx * gd:(gidx + 1) * gd].set(pool_w[i, gidx])
            q_lora = q_b.shape[1]
            qb3 = q_b[i].reshape(q_lora, MLA_HEADS, QK_DIM)
            qb_nope = qb3[:, :, :QK_NOPE].reshape(q_lora, MLA_HEADS * QK_NOPE)
            qb_rope = _pad_lanes(qb3[:, :, QK_NOPE:], LANES).reshape(q_lora, MLA_HEADS * LANES)
            qb_sw = _pad_lanes(_swap_halves(qb3[:, :, QK_NOPE:], negate_first=True), LANES)
            qb = jnp.concatenate([qb_nope, qb_rope, qb_sw.reshape(q_lora, MLA_HEADS * LANES)],
                                 axis=1).astype(BF16)
            w_rope = odd_w_in[i][:, -QK_ROPE:]
            w_in = jnp.concatenate(
                [odd_w_in[i][:, :-QK_ROPE], _pad_lanes(w_rope, LANES),
                 _pad_lanes(_swap_halves(w_rope, negate_first=True), LANES)], axis=1).astype(BF16)
            gain_rows = []
            for gn in (q_norm[i], k_norm[i]):
                g_rope = row(gn[QK_NOPE:])
                gain_rows += [row(gn[:QK_NOPE]), _pad_lanes(g_rope, LANES),
                              _pad_lanes(_swap_halves(g_rope, negate_first=False), LANES)]
            c_out, q, k, v = _odd_pre(
                x, positions, row(mix_norm[layer]), w_in, w_bd.astype(BF16), row(pool_scale[i]),
                row(q_a_norm[i]), qb, row(kv_a_norm[i]), kv_b[i].astype(BF16), *gain_rows, tm=tm)
            d_out = _attention(q, k, v, tq=512, hp=2)
            mixes, w_out = [c_out, d_out], odd_w_out[i].astype(BF16)
        x = _post(x, mixes, w_out, row(ffn_norm[layer]), ffn_w_gate[layer].astype(BF16),
                  ffn_w_up[layer].astype(BF16), ffn_w_down[layer].astype(BF16),
                  tm=tm, ff_chunk=ff_chunk)
    return x
```

```python
import functools
import math

import jax
import jax.numpy as jnp
from jax import lax
from jax.experimental import pallas as pl
from jax.experimental.pallas import tpu as pltpu

F32 = jnp.float32
BF16 = jnp.bfloat16

EPS = 1e-6
MASK_VALUE = -1e30
LANES = 128
MXU_WIDTH = 256
SG_CHUNK = 128
SG_HEADS = 4
CONV_WIDTH = 3
POOL_WINDOWS = (2, 4, 8, 16)
POOL_GROUP_DIM = 64
MLA_HEADS = 6
QK_NOPE = 128
QK_ROPE = 64
QK_DIM = QK_NOPE + QK_ROPE
V_DIM = 128
ROPE_THETA = 10000.0
CARRY_ROWS = 8
POOL_CARRY_ROWS = 16
VMEM_LIMIT_BYTES = 56 * 1024 * 1024


def _rms_norm(x, g):
    ms = jnp.mean(x * x, axis=-1, keepdims=True)
    return x * lax.rsqrt(ms + EPS) * g


def _gelu(x):
    return 0.5 * x * (1.0 + lax.erf(x * math.sqrt(0.5)))


def _dot(a, b):
    return jnp.dot(a, b, preferred_element_type=F32)


def _full_spec(arr):
    nd = arr.ndim
    return pl.BlockSpec(arr.shape, lambda *_: (0,) * nd)


def _params(grid_rank=2):
    return pltpu.CompilerParams(
        dimension_semantics=("arbitrary",) * grid_rank,
        vmem_limit_bytes=VMEM_LIMIT_BYTES)


def _even_mixer_kernel(x_ref, g_ref, w_in_ref, ln_g_ref, w_s_ref, b_st_ref,
                       conv_w_ref, mix_ref, zbuf_ref, *, tm):
    sg_w = SG_HEADS * LANES
    h = _rms_norm(x_ref[...], g_ref[...]).astype(BF16)
    proj = _dot(h, w_in_ref[...])
    u = _gelu(proj[:, 0:sg_w])
    v = _gelu(proj[:, sg_w:2 * sg_w])
    sc_w = (proj.shape[1] - 2 * sg_w) // 3
    b_gate = proj[:, 2 * sg_w:2 * sg_w + sc_w]
    c_gate = proj[:, 2 * sg_w + sc_w:2 * sg_w + 2 * sc_w]
    hv = proj[:, 2 * sg_w + 2 * sc_w:]

    row = lax.broadcasted_iota(jnp.int32, (SG_CHUNK, SG_CHUNK), 0)
    col = lax.broadcasted_iota(jnp.int32, (SG_CHUNK, SG_CHUNK), 1)
    causal = row >= col
    for hd in range(SG_HEADS):
        cs = slice(hd * LANES, (hd + 1) * LANES)
        vh = v[:, cs]
        mu = jnp.mean(vh, axis=-1, keepdims=True)
        xc = vh - mu
        var = jnp.mean(xc * xc, axis=-1, keepdims=True)
        vn = (xc * lax.rsqrt(var + EPS) * ln_g_ref[:, cs]).astype(BF16)
        w = jnp.where(causal, w_s_ref[hd], 0.0).astype(BF16)
        bias = b_st_ref[:, hd:hd + 1]
        for c in range(tm // SG_CHUNK):
            rs = slice(c * SG_CHUNK, (c + 1) * SG_CHUNK)
            mixed = _dot(w, vn[rs]) + bias
            mix_ref[rs, cs] = (u[rs, cs] * mixed).astype(BF16)

    @pl.when(pl.program_id(1) == 0)
    def _():
        zbuf_ref[0:CARRY_ROWS, :] = jnp.zeros((CARRY_ROWS, sc_w), F32)

    z = c_gate * hv
    zbuf_ref[CARRY_ROWS:CARRY_ROWS + tm, :] = z
    y = conv_w_ref[CONV_WIDTH - 1:CONV_WIDTH, :] * z
    for k in range(CONV_WIDTH - 1):
        shift = CONV_WIDTH - 1 - k
        y = y + conv_w_ref[k:k + 1, :] * zbuf_ref[CARRY_ROWS - shift:CARRY_ROWS - shift + tm, :]
    mix_ref[:, sg_w:] = (b_gate * y).astype(BF16)
    zbuf_ref[0:CARRY_ROWS, :] = zbuf_ref[tm:tm + CARRY_ROWS, :]


def _skewed_tile_maps(n_tiles, seq_tiles):
    def lead(t):
        tt = jnp.minimum(t, n_tiles - 1)
        return tt // seq_tiles, lax.rem(tt, seq_tiles)

    def trail(t):
        tt = jnp.maximum(t - 1, 0)
        return tt // seq_tiles, lax.rem(tt, seq_tiles)

    return lead, trail


def _even_mixer(x, g, w_in, ln_g, w_s, b_st, conv_w, *, tm):
    b, s, d = x.shape
    sc_w = conv_w.shape[1]
    mix_w = SG_HEADS * LANES + sc_w
    tok = pl.BlockSpec((None, tm, d), lambda i, j: (i, j, 0))
    return pl.pallas_call(
        functools.partial(_even_mixer_kernel, tm=tm),
        grid=(b, s // tm),
        in_specs=[tok, _full_spec(g), _full_spec(w_in), _full_spec(ln_g),
                  _full_spec(w_s), _full_spec(b_st), _full_spec(conv_w)],
        out_specs=pl.BlockSpec((None, tm, mix_w), lambda i, j: (i, j, 0)),
        out_shape=jax.ShapeDtypeStruct((b, s, mix_w), BF16),
        scratch_shapes=[pltpu.VMEM((tm + CARRY_ROWS, sc_w), F32)],
        compiler_params=_params(),
        name="even_mixer",
    )(x, g, w_in, ln_g, w_s, b_st, conv_w)


def _post_kernel(*refs, n_mix, ff_chunk):
    x_ref = refs[0]
    mix_refs = refs[1:1 + n_mix]
    wo_ref, g_ref, wg_ref, wu_ref, wd_ref, o_ref = refs[1 + n_mix:]
    mix = [m_ref[...] for m_ref in mix_refs]
    mix = mix[0] if n_mix == 1 else jnp.concatenate(mix, axis=1)
    x1 = x_ref[...] + _dot(mix, wo_ref[...])
    h = _rms_norm(x1, g_ref[...]).astype(BF16)
    acc = x1
    d_ff = wg_ref.shape[1]
    for c0 in range(0, d_ff, ff_chunk):
        cs = slice(c0, min(c0 + ff_chunk, d_ff))
        gate = _dot(h, wg_ref[:, cs])
        up = _dot(h, wu_ref[:, cs])
        act = (gate / (1.0 + jnp.exp(-gate)) * up).astype(BF16)
        acc = acc + _dot(act, wd_ref[cs, :])
    o_ref[...] = acc


def _post(x, mixes, w_out, g, wg, wu, wd, *, tm, ff_chunk):
    b, s, d = x.shape
    tok = pl.BlockSpec((None, tm, d), lambda i, j: (i, j, 0))
    mix_specs = [pl.BlockSpec((None, tm, m.shape[2]), lambda i, j: (i, j, 0)) for m in mixes]
    return pl.pallas_call(
        functools.partial(_post_kernel, n_mix=len(mixes), ff_chunk=ff_chunk),
        grid=(b, s // tm),
        in_specs=[tok] + mix_specs
                 + [_full_spec(w_out), _full_spec(g), _full_spec(wg), _full_spec(wu), _full_spec(wd)],
        out_specs=tok,
        out_shape=jax.ShapeDtypeStruct(x.shape, x.dtype),
        compiler_params=_params(),
        name="outproj_ffn",
    )(x, *mixes, w_out, g, wg, wu, wd)


def _rope_tables(pos_ref, invf_ref):
    half = QK_ROPE // 2
    groups = LANES // half
    rows = pos_ref.shape[0]
    pos = pos_ref[...].astype(F32)
    lane = lax.broadcasted_iota(jnp.int32, (rows, LANES), 1)
    p = pos[:, 0:1]
    for grp in range(1, groups):
        p = jnp.where(lane >= grp * half, pos[:, grp:grp + 1], p)
    ang = p * invf_ref[...]
    tables = []
    for packed in (jnp.cos(ang), jnp.sin(ang)):
        quarters = []
        for grp in range(groups):
            r = packed if grp == 0 else pltpu.roll(packed, LANES - grp * half, 1)
            quarters.append(jnp.where(lane < half, r, pltpu.roll(r, half, 1)))
        tables.append(jnp.concatenate(quarters, axis=0))
    return tables


def _shifted(buf_ref, shift, tm, cs):
    return buf_ref[POOL_CARRY_ROWS - shift:POOL_CARRY_ROWS - shift + tm, cs]


POOL_W = len(POOL_WINDOWS) * POOL_GROUP_DIM
ST_ZP = 0
ST_KR = ST_ZP + POOL_W
ST_Q = ST_KR + LANES
ST_KV = ST_Q + 2 * MLA_HEADS * LANES
ST_WIDTH = ST_KV + MLA_HEADS * (QK_NOPE + V_DIM)


def _odd_pre_matmuls(x_ref, g_ref, w_in_ref, qa_g_ref, qb_ref, kva_g_ref, kvb_ref, st_ref):
    q_lora = qa_g_ref.shape[1]
    kv_lora = kva_g_ref.shape[1]
    h = _rms_norm(x_ref[...], g_ref[...]).astype(BF16)
    proj = _dot(h, w_in_ref[...])
    q_lat = proj[:, POOL_W:POOL_W + q_lora]
    kv_lat = proj[:, POOL_W + q_lora:POOL_W + q_lora + kv_lora]
    kr_off = POOL_W + q_lora + kv_lora
    st_ref[:, ST_ZP:ST_KR] = proj[:, 0:POOL_W]
    st_ref[:, ST_KR:ST_Q] = proj[:, kr_off:kr_off + LANES]
    qn = _rms_norm(q_lat, qa_g_ref[...]).astype(BF16)
    st_ref[:, ST_Q:ST_KV] = _dot(qn, qb_ref[...])
    kvn = _rms_norm(kv_lat, kva_g_ref[...]).astype(BF16)
    st_ref[:, ST_KV:ST_WIDTH] = _dot(kvn, kvb_ref[...])


def _odd_pre_tail(st_ref, pos_ref, invf_ref, pool_w_ref, pool_scale_ref, qg_n_ref, qg_r_ref,
                  qg_sw_ref, kg_n_ref, kg_r_ref, kg_sw_ref, c_ref, q_ref, k_ref, v_ref,
                  bufs, seq_tile, tm):
    pool_w = POOL_W
    zbuf_ref, s2buf_ref, s4buf_ref, s8buf_ref = bufs
    zp = st_ref[:, ST_ZP:ST_KR]
    kr = st_ref[:, ST_KR:ST_Q]

    lo, hi = slice(0, LANES), slice(LANES, 2 * LANES)

    for buf in bufs:
        buf[0:POOL_CARRY_ROWS, :] = jnp.where(seq_tile == 0, 0.0, buf[0:POOL_CARRY_ROWS, :])

    body = slice(POOL_CARRY_ROWS, POOL_CARRY_ROWS + tm)
    zbuf_ref[body, :] = zp
    s2 = zp + _shifted(zbuf_ref, 1, tm, slice(0, pool_w))
    s2buf_ref[body, :] = s2
    s4 = s2 + _shifted(s2buf_ref, 2, tm, slice(0, pool_w))
    s4buf_ref[body, :] = s4[:, hi]
    s8 = s4[:, hi] + _shifted(s4buf_ref, 4, tm, lo)
    s8buf_ref[body, :] = s8
    s16 = s8 + _shifted(s8buf_ref, 8, tm, lo)
    for buf in bufs:
        buf[0:POOL_CARRY_ROWS, :] = buf[tm:tm + POOL_CARRY_ROWS, :]
    lane = lax.broadcasted_iota(jnp.int32, (tm, LANES), 1)
    first = lane < POOL_GROUP_DIM
    t1 = (seq_tile * tm + 1
          + lax.broadcasted_iota(jnp.int32, (tm, 1), 0)).astype(F32)
    inv = [1.0 / jnp.minimum(t1, float(w)) for w in POOL_WINDOWS]
    mean_lo = jnp.where(first, s2[:, lo] * inv[0], s4[:, lo] * inv[1])
    mean_hi = jnp.where(first, s8 * inv[2], s16 * inv[3])
    pooled = (jnp.concatenate([mean_lo, mean_hi], axis=1) - zp).astype(BF16)
    c_ref[...] = (_dot(pooled, pool_w_ref[...]) * pool_scale_ref[...]).astype(BF16)

    cosv, sinv = _rope_tables(pos_ref, invf_ref)

    def rope(pair, gain_cos, gain_sin):
        return pair * gain_cos + pltpu.roll(pair, QK_ROPE, 1) * gain_sin

    scale = QK_DIM ** -0.5 * math.log2(math.e)
    rope_w = MLA_HEADS * LANES
    q_gc, q_gs = qg_r_ref[...] * cosv, qg_sw_ref[...] * sinv
    kr_ss = 0.5 * jnp.sum(kr * kr, axis=-1, keepdims=True)
    kr_rot = rope(kr, kg_r_ref[...] * cosv, kg_sw_ref[...] * sinv)
    for hd in range(MLA_HEADS):
        q0 = ST_Q + hd * LANES
        q_n = st_ref[:, q0:q0 + LANES]
        q_pair = st_ref[:, q0 + rope_w:q0 + rope_w + LANES]
        ss = jnp.sum(q_n * q_n + 0.5 * (q_pair * q_pair), axis=-1, keepdims=True)
        rinv = lax.rsqrt(ss * (1.0 / QK_DIM) + EPS) * scale
        q_ref[hd, :, 0:QK_NOPE] = (q_n * qg_n_ref[...] * rinv).astype(BF16)
        q_ref[hd, :, QK_NOPE:] = (rope(q_pair, q_gc, q_gs) * rinv).astype(BF16)

        k0 = ST_KV + hd * (QK_NOPE + V_DIM)
        k_n = st_ref[:, k0:k0 + QK_NOPE]
        ss = jnp.sum(k_n * k_n, axis=-1, keepdims=True) + kr_ss
        rinv = lax.rsqrt(ss * (1.0 / QK_DIM) + EPS)
        k_ref[hd, :, 0:QK_NOPE] = (k_n * kg_n_ref[...] * rinv).astype(BF16)
        k_ref[hd, :, QK_NOPE:] = (kr_rot * rinv).astype(BF16)
        v_ref[hd] = st_ref[:, k0 + QK_NOPE:k0 + QK_NOPE + V_DIM].astype(BF16)


def _odd_pre_kernel(x_ref, pos_ref, invf_ref, g_ref, w_in_ref, pool_w_ref, pool_scale_ref,
                    qa_g_ref, qb_ref, kva_g_ref, kvb_ref, qg_n_ref, qg_r_ref, qg_sw_ref,
                    kg_n_ref, kg_r_ref, kg_sw_ref,
                    c_ref, q_ref, k_ref, v_ref, st_ref, zbuf_ref, s2buf_ref, s4buf_ref, s8buf_ref,
                    *, tm, seq_tiles):
    t = pl.program_id(0)
    seq_tile = lax.rem(jnp.maximum(t - 1, 0), seq_tiles)
    bufs = (zbuf_ref, s2buf_ref, s4buf_ref, s8buf_ref)

    @pl.when(t == 0)
    def _():
        st_ref[1] = jnp.zeros(st_ref.shape[1:], F32)
        for buf in bufs:
            buf[0:POOL_CARRY_ROWS, :] = jnp.zeros((POOL_CARRY_ROWS, buf.shape[1]), F32)

    def step(slot):
        _odd_pre_matmuls(x_ref, g_ref, w_in_ref, qa_g_ref, qb_ref, kva_g_ref, kvb_ref,
                         st_ref.at[slot])
        _odd_pre_tail(st_ref.at[1 - slot], pos_ref, invf_ref, pool_w_ref, pool_scale_ref,
                      qg_n_ref, qg_r_ref, qg_sw_ref, kg_n_ref, kg_r_ref, kg_sw_ref,
                      c_ref, q_ref, k_ref, v_ref, bufs, seq_tile, tm)

    for slot in range(2):
        pl.when(lax.rem(t, 2) == slot)(functools.partial(step, slot))


def _odd_pre(x, positions, g, w_in, pool_w, pool_scale, qa_g, qb, kva_g, kvb,
             qg_n, qg_r, qg_sw, kg_n, kg_r, kg_sw, *, tm):
    b, s, d = x.shape
    pool_width = pool_w.shape[0]
    qk_pad = QK_NOPE + LANES
    groups = LANES // (QK_ROPE // 2)
    seq_tiles = s // tm
    n_tiles = b * seq_tiles
    pos = positions.reshape(b, seq_tiles, groups, tm // groups).transpose(0, 1, 3, 2)
    inv_freq = ROPE_THETA ** (-jnp.arange(0, QK_ROPE, 2, dtype=F32) / QK_ROPE)
    invf = jnp.tile(inv_freq, groups).reshape(1, LANES)

    lead, trail = _skewed_tile_maps(n_tiles, seq_tiles)

    def head_map(t):
        i, j = trail(t)
        return i, 0, j, 0

    consts = [invf, g, w_in, pool_w, pool_scale, qa_g, qb, kva_g, kvb,
              qg_n, qg_r, qg_sw, kg_n, kg_r, kg_sw]
    head_spec = lambda w: pl.BlockSpec((None, MLA_HEADS, tm, w), head_map)
    carry = lambda w: pltpu.VMEM((tm + POOL_CARRY_ROWS, w), F32)
    return pl.pallas_call(
        functools.partial(_odd_pre_kernel, tm=tm, seq_tiles=seq_tiles),
        grid=(n_tiles + 1,),
        in_specs=[pl.BlockSpec((None, tm, d), lambda t: (*lead(t), 0)),
                  pl.BlockSpec((None, None, tm // groups, groups), lambda t: (*trail(t), 0, 0))]
                 + [_full_spec(c) for c in consts],
        out_specs=[pl.BlockSpec((None, tm, pool_width), lambda t: (*trail(t), 0)),
                   head_spec(qk_pad), head_spec(qk_pad), head_spec(V_DIM)],
        out_shape=[jax.ShapeDtypeStruct((b, s, pool_width), BF16),
                   jax.ShapeDtypeStruct((b, MLA_HEADS, s, qk_pad), BF16),
                   jax.ShapeDtypeStruct((b, MLA_HEADS, s, qk_pad), BF16),
                   jax.ShapeDtypeStruct((b, MLA_HEADS, s, V_DIM), BF16)],
        scratch_shapes=[pltpu.VMEM((2, tm, ST_WIDTH), F32),
                        carry(pool_width), carry(pool_width), carry(LANES), carry(LANES)],
        compiler_params=_params(1),
        name="odd_pre",
    )(x, pos, *consts)


def _attn_kernel(q_ref, k_ref, v_ref, o_ref, *, tq, hp):
    s_len = q_ref.shape[1]
    n_col = tq // LANES
    row = lax.broadcasted_iota(jnp.int32, (tq, tq), 0)
    col = lax.broadcasted_iota(jnp.int32, (tq, tq), 1)
    causal = row >= col

    def tile(hd, q0, k0, state):
        q = q_ref[hd, q0:q0 + tq, :]
        k = k_ref[hd, k0:k0 + tq, :]
        v = v_ref[hd, k0:k0 + tq, :]
        sc = lax.dot_general(q, k, (((1,), (1,)), ((), ())), preferred_element_type=F32)
        if k0 == q0:
            sc = jnp.where(causal, sc, MASK_VALUE)
        cols = [sc[:, c * LANES:(c + 1) * LANES] for c in range(n_col)]
        rmax = jnp.max(functools.reduce(jnp.maximum, cols), axis=-1, keepdims=True)
        if state is None:
            m_new = jnp.broadcast_to(rmax, (tq, LANES))
            ps = [jnp.exp2(c - m_new) for c in cols]
            return m_new, functools.reduce(jnp.add, ps), _dot(_bf16_cat(ps), v)
        m_prev, l_prev, acc_prev = state
        m_new = jnp.maximum(m_prev, rmax)
        alpha = jnp.exp2(m_prev - m_new)
        ps = [jnp.exp2(c - m_new) for c in cols]
        return (m_new, alpha * l_prev + functools.reduce(jnp.add, ps),
                alpha * acc_prev + _dot(_bf16_cat(ps), v))

    for qi in range(s_len // tq):
        for hd in range(hp):
            state = None
            for kj in range(qi + 1):
                state = tile(hd, qi * tq, kj * tq, state)
            _, l_part, acc = state
            l = jnp.sum(l_part, axis=-1, keepdims=True)
            o_ref[qi * tq:(qi + 1) * tq, hd * V_DIM:(hd + 1) * V_DIM] = (acc / l).astype(BF16)


def _bf16_cat(cols):
    return jnp.concatenate(cols, axis=1).astype(BF16)


def _attention(q, k, v, *, tq, hp):
    b, nh, s, qk_pad = q.shape
    qk_spec = pl.BlockSpec((None, hp, s, qk_pad), lambda i, j: (i, j, 0, 0))
    return pl.pallas_call(
        functools.partial(_attn_kernel, tq=tq, hp=hp),
        grid=(b, nh // hp),
        in_specs=[qk_spec, qk_spec,
                  pl.BlockSpec((None, hp, s, V_DIM), lambda i, j: (i, j, 0, 0))],
        out_specs=pl.BlockSpec((None, s, hp * V_DIM), lambda i, j: (i, 0, j)),
        out_shape=jax.ShapeDtypeStruct((b, s, nh * V_DIM), BF16),
        compiler_params=_params(),
        name="mla_attention",
    )(q, k, v)


def _pad_lanes(a, width):
    return jnp.pad(a, [(0, 0)] * (a.ndim - 1) + [(0, width - a.shape[-1])])


def _swap_halves(a, *, negate_first):
    a1, a2 = jnp.split(a, 2, axis=-1)
    return jnp.concatenate([-a2 if negate_first else a2, a1], axis=-1)


def _rope_pair(w):
    return jnp.concatenate([w, _swap_halves(w, negate_first=True)], axis=-1)


def kernel(x, positions, mix_norm, ffn_norm, even_w_in, sg_ln_g, sg_w_s, sg_b_s, sc_conv_w,
           even_w_out, odd_w_in, pool_w, pool_scale, q_a_norm, q_b, kv_a_norm, kv_b, q_norm,
           k_norm, odd_w_out, ffn_w_gate, ffn_w_up, ffn_w_down):
    depth = mix_norm.shape[0]
    tm = 512
    ff_chunk = 6 * MXU_WIDTH
    row = lambda a: a.reshape(1, -1)

    for layer in range(depth):
        i = layer // 2
        if layer % 2 == 0:
            mix = _even_mixer(
                x, row(mix_norm[layer]), even_w_in[i].astype(BF16), row(sg_ln_g[i]),
                sg_w_s[i], sg_b_s[i].T, sc_conv_w[i], tm=tm)
            mixes, w_out = [mix], even_w_out[i].astype(BF16)
        else:
            pool_width = pool_scale.shape[-1]
            n_groups = pool_w.shape[1]
            gd = pool_w.shape[2]
            w_bd = jnp.zeros((pool_width, pool_width), F32)
            for gidx in range(n_groups):
                w_bd = w_bd.at[gidx * gd:(gidx + 1) * gd, gidx * gd:(gidx + 1) * gd].set(pool_w[i, gidx])
            q_lora = q_b.shape[1]
            qb3 = q_b[i].reshape(q_lora, MLA_HEADS, QK_DIM)
            qb_nope = qb3[:, :, :QK_NOPE].reshape(q_lora, MLA_HEADS * QK_NOPE)
            qb_pair = _rope_pair(qb3[:, :, QK_NOPE:]).reshape(q_lora, MLA_HEADS * LANES)
            qb = jnp.concatenate([qb_nope, qb_pair], axis=1).astype(BF16)
            w_in = jnp.concatenate(
                [odd_w_in[i][:, :-QK_ROPE], _rope_pair(odd_w_in[i][:, -QK_ROPE:])],
                axis=1).astype(BF16)
            gain_rows = []
            for gn in (q_norm[i], k_norm[i]):
                g_rope = row(gn[QK_NOPE:])
                gain_rows += [row(gn[:QK_NOPE]), _pad_lanes(g_rope, LANES),
                              _pad_lanes(_swap_halves(g_rope, negate_first=False), LANES)]
            c_out, q, k, v = _odd_pre(
                x, positions, row(mix_norm[layer]), w_in, w_bd.astype(BF16), row(pool_scale[i]),
                row(q_a_norm[i]), qb, row(kv_a_norm[i]), kv_b[i].astype(BF16), *gain_rows, tm=tm)
            d_out = _attention(q, k, v, tq=512, hp=2)
            mixes, w_out = [c_out, d_out], odd_w_out[i].astype(BF16)
        x = _post(x, mixes, w_out, row(ffn_norm[layer]), ffn_w_gate[layer].astype(BF16),
                  ffn_w_up[layer].astype(BF16), ffn_w_down[layer].astype(BF16),
                  tm=1024, ff_chunk=3 * MXU_WIDTH)
    return x
```

```python
import functools
import math

import jax
import jax.numpy as jnp
from jax import lax
from jax.experimental import pallas as pl
from jax.experimental.pallas import tpu as pltpu

F32 = jnp.float32
BF16 = jnp.bfloat16

EPS = 1e-6
MASK_VALUE = -1e30
LANES = 128
MXU_WIDTH = 256
SG_CHUNK = 128
SG_HEADS = 4
CONV_WIDTH = 3
POOL_WINDOWS = (2, 4, 8, 16)
POOL_GROUP_DIM = 64
MLA_HEADS = 6
QK_NOPE = 128
QK_ROPE = 64
QK_DIM = QK_NOPE + QK_ROPE
V_DIM = 128
ROPE_THETA = 10000.0
CARRY_ROWS = 8
POOL_CARRY_ROWS = 16
VMEM_LIMIT_BYTES = 58 * 1024 * 1024


def _rms_norm(x, g):
    ms = jnp.mean(x * x, axis=-1, keepdims=True)
    return x * lax.rsqrt(ms + EPS) * g


def _gelu(x):
    return 0.5 * x * (1.0 + lax.erf(x * math.sqrt(0.5)))


def _dot(a, b):
    return jnp.dot(a, b, preferred_element_type=F32)


def _full_spec(arr):
    nd = arr.ndim
    return pl.BlockSpec(arr.shape, lambda *_: (0,) * nd)


def _params(grid_rank=2):
    return pltpu.CompilerParams(
        dimension_semantics=("arbitrary",) * grid_rank,
        vmem_limit_bytes=VMEM_LIMIT_BYTES)


def _even_mixer_kernel(x_ref, g_ref, w_in_ref, ln_g_ref, w_s_ref, b_st_ref,
                       conv_w_ref, mix_ref, zbuf_ref, *, tm):
    sg_w = SG_HEADS * LANES
    h = _rms_norm(x_ref[...], g_ref[...]).astype(BF16)
    proj = _dot(h, w_in_ref[...])
    u = _gelu(proj[:, 0:sg_w])
    v = _gelu(proj[:, sg_w:2 * sg_w])
    sc_w = (proj.shape[1] - 2 * sg_w) // 3
    b_gate = proj[:, 2 * sg_w:2 * sg_w + sc_w]
    c_gate = proj[:, 2 * sg_w + sc_w:2 * sg_w + 2 * sc_w]
    hv = proj[:, 2 * sg_w + 2 * sc_w:]

    row = lax.broadcasted_iota(jnp.int32, (SG_CHUNK, SG_CHUNK), 0)
    col = lax.broadcasted_iota(jnp.int32, (SG_CHUNK, SG_CHUNK), 1)
    causal = row >= col
    for hd in range(SG_HEADS):
        cs = slice(hd * LANES, (hd + 1) * LANES)
        vh = v[:, cs]
        mu = jnp.mean(vh, axis=-1, keepdims=True)
        xc = vh - mu
        var = jnp.mean(xc * xc, axis=-1, keepdims=True)
        vn = (xc * lax.rsqrt(var + EPS) * ln_g_ref[:, cs]).astype(BF16)
        w = jnp.where(causal, w_s_ref[hd], 0.0).astype(BF16)
        bias = b_st_ref[:, hd:hd + 1]
        for c in range(tm // SG_CHUNK):
            rs = slice(c * SG_CHUNK, (c + 1) * SG_CHUNK)
            mixed = _dot(w, vn[rs]) + bias
            mix_ref[rs, cs] = (u[rs, cs] * mixed).astype(BF16)

    @pl.when(pl.program_id(1) == 0)
    def _():
        zbuf_ref[0:CARRY_ROWS, :] = jnp.zeros((CARRY_ROWS, sc_w), F32)

    z = c_gate * hv
    zbuf_ref[CARRY_ROWS:CARRY_ROWS + tm, :] = z
    y = conv_w_ref[CONV_WIDTH - 1:CONV_WIDTH, :] * z
    for k in range(CONV_WIDTH - 1):
        shift = CONV_WIDTH - 1 - k
        y = y + conv_w_ref[k:k + 1, :] * zbuf_ref[CARRY_ROWS - shift:CARRY_ROWS - shift + tm, :]
    mix_ref[:, sg_w:] = (b_gate * y).astype(BF16)
    zbuf_ref[0:CARRY_ROWS, :] = zbuf_ref[tm:tm + CARRY_ROWS, :]


def _skewed_tile_maps(n_tiles, seq_tiles):
    def lead(t):
        tt = jnp.minimum(t, n_tiles - 1)
        return tt // seq_tiles, lax.rem(tt, seq_tiles)

    def trail(t):
        tt = jnp.maximum(t - 1, 0)
        return tt // seq_tiles, lax.rem(tt, seq_tiles)

    return lead, trail


def _even_mixer(x, g, w_in, ln_g, w_s, b_st, conv_w, *, tm):
    b, s, d = x.shape
    sc_w = conv_w.shape[1]
    mix_w = SG_HEADS * LANES + sc_w
    tok = pl.BlockSpec((None, tm, d), lambda i, j: (i, j, 0))
    return pl.pallas_call(
        functools.partial(_even_mixer_kernel, tm=tm),
        grid=(b, s // tm),
        in_specs=[tok, _full_spec(g), _full_spec(w_in), _full_spec(ln_g),
                  _full_spec(w_s), _full_spec(b_st), _full_spec(conv_w)],
        out_specs=pl.BlockSpec((None, tm, mix_w), lambda i, j: (i, j, 0)),
        out_shape=jax.ShapeDtypeStruct((b, s, mix_w), BF16),
        scratch_shapes=[pltpu.VMEM((tm + CARRY_ROWS, sc_w), F32)],
        compiler_params=_params(),
        name="even_mixer",
    )(x, g, w_in, ln_g, w_s, b_st, conv_w)


W_CAST_STEPS = 16


def _post_kernel(*refs, n_mix, ff_chunk):
    x_ref = refs[0]
    mix_refs = refs[1:1 + n_mix]
    (wo_ref, g_ref, wg_ref, wu_ref, wd_ref, o_ref,
     wo_s, wg_s, wu_s, wd_s) = refs[1 + n_mix:]
    step = pl.program_id(0)

    @pl.when(step < W_CAST_STEPS)
    def _():
        for src, dst in ((wo_ref, wo_s), (wg_ref, wg_s), (wu_ref, wu_s), (wd_ref, wd_s)):
            rows = src.shape[0]
            dst[pl.ds(pl.multiple_of(step * rows, rows), rows), :] = src[...].astype(BF16)

    @pl.when(step >= W_CAST_STEPS)
    def _():
        mix = [m_ref[...] for m_ref in mix_refs]
        mix = mix[0] if n_mix == 1 else jnp.concatenate(mix, axis=1)
        x1 = x_ref[...] + _dot(mix, wo_s[...])
        h = _rms_norm(x1, g_ref[...]).astype(BF16)
        acc = x1
        d_ff = wg_s.shape[1]
        for c0 in range(0, d_ff, ff_chunk):
            cs = slice(c0, min(c0 + ff_chunk, d_ff))
            gate = _dot(h, wg_s[:, cs])
            up = _dot(h, wu_s[:, cs])
            act = (gate / (1.0 + jnp.exp(-gate)) * up).astype(BF16)
            acc = acc + _dot(act, wd_s[cs, :])
        o_ref[...] = acc


def _post(x, mixes, w_out, out_layer, g, wg, wu, wd, layer, *, tm, ff_chunk):
    b, s, d = x.shape
    seq_tiles = s // tm

    def tile(t):
        tt = jnp.maximum(t - W_CAST_STEPS, 0)
        return tt // seq_tiles, lax.rem(tt, seq_tiles)

    def chunk_spec(w, lyr):
        rows = w.shape[1] // W_CAST_STEPS
        return pl.BlockSpec((None, rows, w.shape[2]),
                            lambda t: (lyr, jnp.minimum(t, W_CAST_STEPS - 1), 0))

    tok = pl.BlockSpec((None, tm, d), lambda t: (*tile(t), 0))
    mix_specs = [pl.BlockSpec((None, tm, m.shape[2]), lambda t: (*tile(t), 0)) for m in mixes]
    return pl.pallas_call(
        functools.partial(_post_kernel, n_mix=len(mixes), ff_chunk=ff_chunk),
        grid=(W_CAST_STEPS + b * seq_tiles,),
        in_specs=[tok] + mix_specs
                 + [chunk_spec(w_out, out_layer), _full_spec(g), chunk_spec(wg, layer),
                    chunk_spec(wu, layer), chunk_spec(wd, layer)],
        out_specs=tok,
        out_shape=jax.ShapeDtypeStruct(x.shape, x.dtype),
        scratch_shapes=[pltpu.VMEM(w.shape[1:], BF16) for w in (w_out, wg, wu, wd)],
        compiler_params=_params(1),
        name="outproj_ffn",
    )(x, *mixes, w_out, g, wg, wu, wd)


def _rope_tables(pos_ref, invf_ref):
    half = QK_ROPE // 2
    groups = LANES // half
    rows = pos_ref.shape[0]
    pos = pos_ref[...].astype(F32)
    lane = lax.broadcasted_iota(jnp.int32, (rows, LANES), 1)
    p = pos[:, 0:1]
    for grp in range(1, groups):
        p = jnp.where(lane >= grp * half, pos[:, grp:grp + 1], p)
    ang = p * invf_ref[...]
    tables = []
    for packed in (jnp.cos(ang), jnp.sin(ang)):
        quarters = []
        for grp in range(groups):
            r = packed if grp == 0 else pltpu.roll(packed, LANES - grp * half, 1)
            quarters.append(jnp.where(lane < half, r, pltpu.roll(r, half, 1)))
        tables.append(jnp.concatenate(quarters, axis=0))
    return tables


def _shifted(buf_ref, shift, tm, cs):
    return buf_ref[POOL_CARRY_ROWS - shift:POOL_CARRY_ROWS - shift + tm, cs]


POOL_W = len(POOL_WINDOWS) * POOL_GROUP_DIM
ST_ZP = 0
ST_KR = ST_ZP + POOL_W
ST_Q = ST_KR + LANES
ST_KV = ST_Q + 2 * MLA_HEADS * LANES
ST_WIDTH = ST_KV + MLA_HEADS * (QK_NOPE + V_DIM)


def _odd_pre_matmuls(x_ref, g_ref, w_in_ref, qa_g_ref, qb_ref, kva_g_ref, kvb_ref, st_ref):
    q_lora = qa_g_ref.shape[1]
    kv_lora = kva_g_ref.shape[1]
    h = _rms_norm(x_ref[...], g_ref[...]).astype(BF16)
    proj = _dot(h, w_in_ref[...])
    q_lat = proj[:, POOL_W:POOL_W + q_lora]
    kv_lat = proj[:, POOL_W + q_lora:POOL_W + q_lora + kv_lora]
    kr_off = POOL_W + q_lora + kv_lora
    st_ref[:, ST_ZP:ST_KR] = proj[:, 0:POOL_W]
    st_ref[:, ST_KR:ST_Q] = proj[:, kr_off:kr_off + LANES]
    qn = _rms_norm(q_lat, qa_g_ref[...]).astype(BF16)
    st_ref[:, ST_Q:ST_KV] = _dot(qn, qb_ref[...])
    kvn = _rms_norm(kv_lat, kva_g_ref[...]).astype(BF16)
    st_ref[:, ST_KV:ST_WIDTH] = _dot(kvn, kvb_ref[...])


def _odd_pre_tail(st_ref, pos_ref, invf_ref, pool_w_ref, pool_scale_ref, qg_n_ref, qg_r_ref,
                  qg_sw_ref, kg_n_ref, kg_r_ref, kg_sw_ref, c_ref, q_ref, k_ref, v_ref,
                  bufs, seq_tile, tm):
    pool_w = POOL_W
    zbuf_ref, s2buf_ref, s4buf_ref, s8buf_ref = bufs
    zp = st_ref[:, ST_ZP:ST_KR]
    kr = st_ref[:, ST_KR:ST_Q]

    lo, hi = slice(0, LANES), slice(LANES, 2 * LANES)

    for buf in bufs:
        buf[0:POOL_CARRY_ROWS, :] = jnp.where(seq_tile == 0, 0.0, buf[0:POOL_CARRY_ROWS, :])

    body = slice(POOL_CARRY_ROWS, POOL_CARRY_ROWS + tm)
    zbuf_ref[body, :] = zp
    s2 = zp + _shifted(zbuf_ref, 1, tm, slice(0, pool_w))
    s2buf_ref[body, :] = s2
    s4 = s2 + _shifted(s2buf_ref, 2, tm, slice(0, pool_w))
    s4buf_ref[body, :] = s4[:, hi]
    s8 = s4[:, hi] + _shifted(s4buf_ref, 4, tm, lo)
    s8buf_ref[body, :] = s8
    s16 = s8 + _shifted(s8buf_ref, 8, tm, lo)
    for buf in bufs:
        buf[0:POOL_CARRY_ROWS, :] = buf[tm:tm + POOL_CARRY_ROWS, :]
    lane = lax.broadcasted_iota(jnp.int32, (tm, LANES), 1)
    first = lane < POOL_GROUP_DIM
    t1 = (seq_tile * tm + 1
          + lax.broadcasted_iota(jnp.int32, (tm, 1), 0)).astype(F32)
    inv = [1.0 / jnp.minimum(t1, float(w)) for w in POOL_WINDOWS]
    mean_lo = jnp.where(first, s2[:, lo] * inv[0], s4[:, lo] * inv[1])
    mean_hi = jnp.where(first, s8 * inv[2], s16 * inv[3])
    pooled = (jnp.concatenate([mean_lo, mean_hi], axis=1) - zp).astype(BF16)
    c_ref[...] = (_dot(pooled, pool_w_ref[...]) * pool_scale_ref[...]).astype(BF16)

    cosv, sinv = _rope_tables(pos_ref, invf_ref)

    def rope(pair, gain_cos, gain_sin):
        return pair * gain_cos + pltpu.roll(pair, QK_ROPE, 1) * gain_sin

    scale = QK_DIM ** -0.5 * math.log2(math.e)
    rope_w = MLA_HEADS * LANES
    q_gc, q_gs = qg_r_ref[...] * cosv, qg_sw_ref[...] * sinv
    kr_ss = 0.5 * jnp.sum(kr * kr, axis=-1, keepdims=True)
    kr_rot = rope(kr, kg_r_ref[...] * cosv, kg_sw_ref[...] * sinv)
    for hd in range(MLA_HEADS):
        q0 = ST_Q + hd * LANES
        q_n = st_ref[:, q0:q0 + LANES]
        q_pair = st_ref[:, q0 + rope_w:q0 + rope_w + LANES]
        ss = jnp.sum(q_n * q_n + 0.5 * (q_pair * q_pair), axis=-1, keepdims=True)
        rinv = lax.rsqrt(ss * (1.0 / QK_DIM) + EPS) * scale
        q_ref[hd, :, 0:QK_NOPE] = (q_n * qg_n_ref[...] * rinv).astype(BF16)
        q_ref[hd, :, QK_NOPE:] = (rope(q_pair, q_gc, q_gs) * rinv).astype(BF16)

        k0 = ST_KV + hd * (QK_NOPE + V_DIM)
        k_n = st_ref[:, k0:k0 + QK_NOPE]
        ss = jnp.sum(k_n * k_n, axis=-1, keepdims=True) + kr_ss
        rinv = lax.rsqrt(ss * (1.0 / QK_DIM) + EPS)
        k_ref[hd, :, 0:QK_NOPE] = (k_n * kg_n_ref[...] * rinv).astype(BF16)
        k_ref[hd, :, QK_NOPE:] = (kr_rot * rinv).astype(BF16)
        v_ref[hd] = st_ref[:, k0 + QK_NOPE:k0 + QK_NOPE + V_DIM].astype(BF16)


def _odd_pre_kernel(x_ref, pos_ref, invf_ref, g_ref, w_in_ref, pool_w_ref, pool_scale_ref,
                    qa_g_ref, qb_ref, kva_g_ref, kvb_ref, qg_n_ref, qg_r_ref, qg_sw_ref,
                    kg_n_ref, kg_r_ref, kg_sw_ref,
                    c_ref, q_ref, k_ref, v_ref, st_ref, zbuf_ref, s2buf_ref, s4buf_ref, s8buf_ref,
                    *, tm, seq_tiles):
    t = pl.program_id(0)
    seq_tile = lax.rem(jnp.maximum(t - 1, 0), seq_tiles)
    bufs = (zbuf_ref, s2buf_ref, s4buf_ref, s8buf_ref)

    @pl.when(t == 0)
    def _():
        st_ref[1] = jnp.zeros(st_ref.shape[1:], F32)
        for buf in bufs:
            buf[0:POOL_CARRY_ROWS, :] = jnp.zeros((POOL_CARRY_ROWS, buf.shape[1]), F32)

    def step(slot):
        _odd_pre_matmuls(x_ref, g_ref, w_in_ref, qa_g_ref, qb_ref, kva_g_ref, kvb_ref,
                         st_ref.at[slot])
        _odd_pre_tail(st_ref.at[1 - slot], pos_ref, invf_ref, pool_w_ref, pool_scale_ref,
                      qg_n_ref, qg_r_ref, qg_sw_ref, kg_n_ref, kg_r_ref, kg_sw_ref,
                      c_ref, q_ref, k_ref, v_ref, bufs, seq_tile, tm)

    for slot in range(2):
        pl.when(lax.rem(t, 2) == slot)(functools.partial(step, slot))


def _odd_pre(x, positions, g, w_in, pool_w, pool_scale, qa_g, qb, kva_g, kvb,
             qg_n, qg_r, qg_sw, kg_n, kg_r, kg_sw, *, tm):
    b, s, d = x.shape
    pool_width = pool_w.shape[0]
    qk_pad = QK_NOPE + LANES
    groups = LANES // (QK_ROPE // 2)
    seq_tiles = s // tm
    n_tiles = b * seq_tiles
    pos = positions.reshape(b, seq_tiles, groups, tm // groups).transpose(0, 1, 3, 2)
    inv_freq = ROPE_THETA ** (-jnp.arange(0, QK_ROPE, 2, dtype=F32) / QK_ROPE)
    invf = jnp.tile(inv_freq, groups).reshape(1, LANES)

    lead, trail = _skewed_tile_maps(n_tiles, seq_tiles)

    def head_map(t):
        i, j = trail(t)
        return i, 0, j, 0

    consts = [invf, g, w_in, pool_w, pool_scale, qa_g, qb, kva_g, kvb,
              qg_n, qg_r, qg_sw, kg_n, kg_r, kg_sw]
    head_spec = lambda w: pl.BlockSpec((None, MLA_HEADS, tm, w), head_map)
    carry = lambda w: pltpu.VMEM((tm + POOL_CARRY_ROWS, w), F32)
    return pl.pallas_call(
        functools.partial(_odd_pre_kernel, tm=tm, seq_tiles=seq_tiles),
        grid=(n_tiles + 1,),
        in_specs=[pl.BlockSpec((None, tm, d), lambda t: (*lead(t), 0)),
                  pl.BlockSpec((None, None, tm // groups, groups), lambda t: (*trail(t), 0, 0))]
                 + [_full_spec(c) for c in consts],
        out_specs=[pl.BlockSpec((None, tm, pool_width), lambda t: (*trail(t), 0)),
                   head_spec(qk_pad), head_spec(qk_pad), head_spec(V_DIM)],
        out_shape=[jax.ShapeDtypeStruct((b, s, pool_width), BF16),
                   jax.ShapeDtypeStruct((b, MLA_HEADS, s, qk_pad), BF16),
                   jax.ShapeDtypeStruct((b, MLA_HEADS, s, qk_pad), BF16),
                   jax.ShapeDtypeStruct((b, MLA_HEADS, s, V_DIM), BF16)],
        scratch_shapes=[pltpu.VMEM((2, tm, ST_WIDTH), F32),
                        carry(pool_width), carry(pool_width), carry(LANES), carry(LANES)],
        compiler_params=_params(1),
        name="odd_pre",
    )(x, pos, *consts)


def _attn_kernel(q_ref, k_ref, v_ref, o_ref, *, tq, hp):
    s_len = q_ref.shape[1]
    n_col = tq // LANES
    row = lax.broadcasted_iota(jnp.int32, (tq, tq), 0)
    col = lax.broadcasted_iota(jnp.int32, (tq, tq), 1)
    causal = row >= col

    def tile(hd, q0, k0, state):
        q = q_ref[hd, q0:q0 + tq, :]
        k = k_ref[hd, k0:k0 + tq, :]
        v = v_ref[hd, k0:k0 + tq, :]
        sc = lax.dot_general(q, k, (((1,), (1,)), ((), ())), preferred_element_type=F32)
        if k0 == q0:
            sc = jnp.where(causal, sc, MASK_VALUE)
        cols = [sc[:, c * LANES:(c + 1) * LANES] for c in range(n_col)]
        rmax = jnp.max(functools.reduce(jnp.maximum, cols), axis=-1, keepdims=True)
        if state is None:
            m_new = jnp.broadcast_to(rmax, (tq, LANES))
            ps = [jnp.exp2(c - m_new) for c in cols]
            return m_new, functools.reduce(jnp.add, ps), _dot(_bf16_cat(ps), v)
        m_prev, l_prev, acc_prev = state
        m_new = jnp.maximum(m_prev, rmax)
        alpha = jnp.exp2(m_prev - m_new)
        ps = [jnp.exp2(c - m_new) for c in cols]
        return (m_new, alpha * l_prev + functools.reduce(jnp.add, ps),
                alpha * acc_prev + _dot(_bf16_cat(ps), v))

    for qi in range(s_len // tq):
        for hd in range(hp):
            state = None
            for kj in range(qi + 1):
                state = tile(hd, qi * tq, kj * tq, state)
            _, l_part, acc = state
            l = jnp.sum(l_part, axis=-1, keepdims=True)
            o_ref[qi * tq:(qi + 1) * tq, hd * V_DIM:(hd + 1) * V_DIM] = (acc / l).astype(BF16)


def _bf16_cat(cols):
    return jnp.concatenate(cols, axis=1).astype(BF16)


def _attention(q, k, v, *, tq, hp):
    b, nh, s, qk_pad = q.shape
    qk_spec = pl.BlockSpec((None, hp, s, qk_pad), lambda i, j: (i, j, 0, 0))
    return pl.pallas_call(
        functools.partial(_attn_kernel, tq=tq, hp=hp),
        grid=(b, nh // hp),
        in_specs=[qk_spec, qk_spec,
                  pl.BlockSpec((None, hp, s, V_DIM), lambda i, j: (i, j, 0, 0))],
        out_specs=pl.BlockSpec((None, s, hp * V_DIM), lambda i, j: (i, 0, j)),
        out_shape=jax.ShapeDtypeStruct((b, s, nh * V_DIM), BF16),
        compiler_params=_params(),
        name="mla_attention",
    )(q, k, v)


def _pad_lanes(a, width):
    return jnp.pad(a, [(0, 0)] * (a.ndim - 1) + [(0, width - a.shape[-1])])


def _swap_halves(a, *, negate_first):
    a1, a2 = jnp.split(a, 2, axis=-1)
    return jnp.concatenate([-a2 if negate_first else a2, a1], axis=-1)


def _rope_pair(w):
    return jnp.concatenate([w, _swap_halves(w, negate_first=True)], axis=-1)


def kernel(x, positions, mix_norm, ffn_norm, even_w_in, sg_ln_g, sg_w_s, sg_b_s, sc_conv_w,
           even_w_out, odd_w_in, pool_w, pool_scale, q_a_norm, q_b, kv_a_norm, kv_b, q_norm,
           k_norm, odd_w_out, ffn_w_gate, ffn_w_up, ffn_w_down):
    depth = mix_norm.shape[0]
    tm = 512
    ff_chunk = 6 * MXU_WIDTH
    row = lambda a: a.reshape(1, -1)

    for layer in range(depth):
        i = layer // 2
        if layer % 2 == 0:
            mix = _even_mixer(
                x, row(mix_norm[layer]), even_w_in[i].astype(BF16), row(sg_ln_g[i]),
                sg_w_s[i], sg_b_s[i].T, sc_conv_w[i], tm=tm)
            mixes, w_out = [mix], even_w_out
        else:
            pool_width = pool_scale.shape[-1]
            n_groups = pool_w.shape[1]
            gd = pool_w.shape[2]
            w_bd = jnp.zeros((pool_width, pool_width), F32)
            for gidx in range(n_groups):
                w_bd = w_bd.at[gidx * gd:(gidx + 1) * gd, gidx * gd:(gidx + 1) * gd].set(pool_w[i, gidx])
            q_lora = q_b.shape[1]
            qb3 = q_b[i].reshape(q_lora, MLA_HEADS, QK_DIM)
            qb_nope = qb3[:, :, :QK_NOPE].reshape(q_lora, MLA_HEADS * QK_NOPE)
            qb_pair = _rope_pair(qb3[:, :, QK_NOPE:]).reshape(q_lora, MLA_HEADS * LANES)
            qb = jnp.concatenate([qb_nope, qb_pair], axis=1).astype(BF16)
            w_in = jnp.concatenate(
                [odd_w_in[i][:, :-QK_ROPE], _rope_pair(odd_w_in[i][:, -QK_ROPE:])],
                axis=1).astype(BF16)
            gain_rows = []
            for gn in (q_norm[i], k_norm[i]):
                g_rope = row(gn[QK_NOPE:])
                gain_rows += [row(gn[:QK_NOPE]), _pad_lanes(g_rope, LANES),
                              _pad_lanes(_swap_halves(g_rope, negate_first=False), LANES)]
            c_out, q, k, v = _odd_pre(
                x, positions, row(mix_norm[layer]), w_in, w_bd.astype(BF16), row(pool_scale[i]),
                row(q_a_norm[i]), qb, row(kv_a_norm[i]), kv_b[i].astype(BF16), *gain_rows, tm=tm)
            d_out = _attention(q, k, v, tq=512, hp=2)
            mixes, w_out = [c_out, d_out], odd_w_out
        x = _post(x, mixes, w_out, i, row(ffn_norm[layer]), ffn_w_gate, ffn_w_up, ffn_w_down,
                  layer, tm=1024, ff_chunk=2 * MXU_WIDTH)
    return x
```

```python
import functools
import math
from typing import NamedTuple

import jax
import jax.numpy as jnp
from jax import lax
from jax.experimental import pallas as pl
from jax.experimental.pallas import tpu as pltpu

F32 = jnp.float32
BF16 = jnp.bfloat16

EPS = 1e-6
MASK_VALUE = -1e30
LANES = 128
MXU_WIDTH = 256
SG_CHUNK = 128
SG_HEADS = 4
CONV_WIDTH = 3
POOL_WINDOWS = (2, 4, 8, 16)
POOL_GROUP_DIM = 64
MLA_HEADS = 6
QK_NOPE = 128
QK_ROPE = 64
QK_DIM = QK_NOPE + QK_ROPE
V_DIM = 128
ROPE_THETA = 10000.0
CARRY_ROWS = 8
POOL_CARRY_ROWS = 16
VMEM_LIMIT_BYTES = 58 * 1024 * 1024


def _rms_norm(x, g):
    ms = jnp.mean(x * x, axis=-1, keepdims=True)
    return x * lax.rsqrt(ms + EPS) * g


def _gelu(x):
    return 0.5 * x * (1.0 + lax.erf(x * math.sqrt(0.5)))


def _dot(a, b):
    return jnp.dot(a, b, preferred_element_type=F32)


def _full_spec(arr):
    nd = arr.ndim
    return pl.BlockSpec(arr.shape, lambda *_: (0,) * nd)


def _params(grid_rank=2):
    return pltpu.CompilerParams(
        dimension_semantics=("arbitrary",) * grid_rank,
        vmem_limit_bytes=VMEM_LIMIT_BYTES)


def _even_mixer_kernel(x_ref, g_ref, w_in_ref, ln_g_ref, w_s_ref, b_st_ref,
                       conv_w_ref, mix_ref, zbuf_ref, *, tm):
    sg_w = SG_HEADS * LANES
    h = _rms_norm(x_ref[...], g_ref[...]).astype(BF16)
    proj = _dot(h, w_in_ref[...])
    u = _gelu(proj[:, 0:sg_w])
    v = _gelu(proj[:, sg_w:2 * sg_w])
    sc_w = (proj.shape[1] - 2 * sg_w) // 3
    b_gate = proj[:, 2 * sg_w:2 * sg_w + sc_w]
    c_gate = proj[:, 2 * sg_w + sc_w:2 * sg_w + 2 * sc_w]
    hv = proj[:, 2 * sg_w + 2 * sc_w:]

    row = lax.broadcasted_iota(jnp.int32, (SG_CHUNK, SG_CHUNK), 0)
    col = lax.broadcasted_iota(jnp.int32, (SG_CHUNK, SG_CHUNK), 1)
    causal = row >= col
    for hd in range(SG_HEADS):
        cs = slice(hd * LANES, (hd + 1) * LANES)
        vh = v[:, cs]
        mu = jnp.mean(vh, axis=-1, keepdims=True)
        xc = vh - mu
        var = jnp.mean(xc * xc, axis=-1, keepdims=True)
        vn = (xc * lax.rsqrt(var + EPS) * ln_g_ref[:, cs]).astype(BF16)
        w = jnp.where(causal, w_s_ref[hd], 0.0).astype(BF16)
        bias = b_st_ref[:, hd:hd + 1]
        for c in range(tm // SG_CHUNK):
            rs = slice(c * SG_CHUNK, (c + 1) * SG_CHUNK)
            mixed = _dot(w, vn[rs]) + bias
            mix_ref[rs, cs] = (u[rs, cs] * mixed).astype(BF16)

    @pl.when(pl.program_id(1) == 0)
    def _():
        zbuf_ref[0:CARRY_ROWS, :] = jnp.zeros((CARRY_ROWS, sc_w), F32)

    z = c_gate * hv
    zbuf_ref[CARRY_ROWS:CARRY_ROWS + tm, :] = z
    y = conv_w_ref[CONV_WIDTH - 1:CONV_WIDTH, :] * z
    for k in range(CONV_WIDTH - 1):
        shift = CONV_WIDTH - 1 - k
        y = y + conv_w_ref[k:k + 1, :] * zbuf_ref[CARRY_ROWS - shift:CARRY_ROWS - shift + tm, :]
    mix_ref[:, sg_w:] = (b_gate * y).astype(BF16)
    zbuf_ref[0:CARRY_ROWS, :] = zbuf_ref[tm:tm + CARRY_ROWS, :]


def _skewed_tile_maps(n_tiles, seq_tiles):
    def lead(t):
        tt = jnp.minimum(t, n_tiles - 1)
        return tt // seq_tiles, lax.rem(tt, seq_tiles)

    def trail(t):
        tt = jnp.maximum(t - 1, 0)
        return tt // seq_tiles, lax.rem(tt, seq_tiles)

    return lead, trail


def _even_mixer(x, g, w_in, ln_g, w_s, b_st, conv_w, *, tm):
    b, s, d = x.shape
    sc_w = conv_w.shape[1]
    mix_w = SG_HEADS * LANES + sc_w
    tok = pl.BlockSpec((None, tm, d), lambda i, j: (i, j, 0))
    return pl.pallas_call(
        functools.partial(_even_mixer_kernel, tm=tm),
        grid=(b, s // tm),
        in_specs=[tok, _full_spec(g), _full_spec(w_in), _full_spec(ln_g),
                  _full_spec(w_s), _full_spec(b_st), _full_spec(conv_w)],
        out_specs=pl.BlockSpec((None, tm, mix_w), lambda i, j: (i, j, 0)),
        out_shape=jax.ShapeDtypeStruct((b, s, mix_w), BF16),
        scratch_shapes=[pltpu.VMEM((tm + CARRY_ROWS, sc_w), F32)],
        compiler_params=_params(),
        name="even_mixer",
    )(x, g, w_in, ln_g, w_s, b_st, conv_w)


W_CAST_STEPS = 16


def _post_kernel(*refs, n_mix, ff_chunk):
    x_ref = refs[0]
    mix_refs = refs[1:1 + n_mix]
    (wo_ref, g_ref, wg_ref, wu_ref, wd_ref, o_ref,
     wo_s, wg_s, wu_s, wd_s) = refs[1 + n_mix:]
    step = pl.program_id(0)

    @pl.when(step < W_CAST_STEPS)
    def _():
        for src, dst in ((wo_ref, wo_s), (wg_ref, wg_s), (wu_ref, wu_s), (wd_ref, wd_s)):
            rows = src.shape[0]
            dst[pl.ds(pl.multiple_of(step * rows, rows), rows), :] = src[...].astype(BF16)

    @pl.when(step >= W_CAST_STEPS)
    def _():
        mix = [m_ref[...] for m_ref in mix_refs]
        mix = mix[0] if n_mix == 1 else jnp.concatenate(mix, axis=1)
        x1 = x_ref[...] + _dot(mix, wo_s[...])
        h = _rms_norm(x1, g_ref[...]).astype(BF16)
        acc = x1
        d_ff = wg_s.shape[1]
        for c0 in range(0, d_ff, ff_chunk):
            cs = slice(c0, min(c0 + ff_chunk, d_ff))
            gate = _dot(h, wg_s[:, cs])
            up = _dot(h, wu_s[:, cs])
            act = (gate / (1.0 + jnp.exp(-gate)) * up).astype(BF16)
            acc = acc + _dot(act, wd_s[cs, :])
        o_ref[...] = acc


def _post(x, mixes, w_out, out_layer, g, wg, wu, wd, layer, *, tm, ff_chunk):
    b, s, d = x.shape
    seq_tiles = s // tm

    def tile(t):
        tt = jnp.maximum(t - W_CAST_STEPS, 0)
        return tt // seq_tiles, lax.rem(tt, seq_tiles)

    def chunk_spec(w, lyr):
        rows = w.shape[1] // W_CAST_STEPS
        return pl.BlockSpec((None, rows, w.shape[2]),
                            lambda t: (lyr, jnp.minimum(t, W_CAST_STEPS - 1), 0))

    tok = pl.BlockSpec((None, tm, d), lambda t: (*tile(t), 0))
    mix_specs = [pl.BlockSpec((None, tm, m.shape[2]), lambda t: (*tile(t), 0)) for m in mixes]
    return pl.pallas_call(
        functools.partial(_post_kernel, n_mix=len(mixes), ff_chunk=ff_chunk),
        grid=(W_CAST_STEPS + b * seq_tiles,),
        in_specs=[tok] + mix_specs
                 + [chunk_spec(w_out, out_layer), _full_spec(g), chunk_spec(wg, layer),
                    chunk_spec(wu, layer), chunk_spec(wd, layer)],
        out_specs=tok,
        out_shape=jax.ShapeDtypeStruct(x.shape, x.dtype),
        scratch_shapes=[pltpu.VMEM(w.shape[1:], BF16) for w in (w_out, wg, wu, wd)],
        compiler_params=_params(1),
        name="outproj_ffn",
    )(x, *mixes, w_out, g, wg, wu, wd)


def _rope_tables(pos_ref, invf_ref):
    half = QK_ROPE // 2
    groups = LANES // half
    rows = pos_ref.shape[0]
    pos = pos_ref[...].astype(F32)
    lane = lax.broadcasted_iota(jnp.int32, (rows, LANES), 1)
    p = pos[:, 0:1]
    for grp in range(1, groups):
        p = jnp.where(lane >= grp * half, pos[:, grp:grp + 1], p)
    ang = p * invf_ref[...]
    tables = []
    for packed in (jnp.cos(ang), jnp.sin(ang)):
        quarters = []
        for grp in range(groups):
            r = packed if grp == 0 else pltpu.roll(packed, LANES - grp * half, 1)
            quarters.append(jnp.where(lane < half, r, pltpu.roll(r, half, 1)))
        tables.append(jnp.concatenate(quarters, axis=0))
    return tables


def _shifted(buf_ref, shift, tm, cs):
    return buf_ref[POOL_CARRY_ROWS - shift:POOL_CARRY_ROWS - shift + tm, cs]


POOL_W = len(POOL_WINDOWS) * POOL_GROUP_DIM
ST_ZP = 0
ST_KR = ST_ZP + POOL_W
ST_Q = ST_KR + LANES
ST_KV = ST_Q + 2 * MLA_HEADS * LANES
ST_WIDTH = ST_KV + MLA_HEADS * (QK_NOPE + V_DIM)


def _odd_pre_matmuls(x_ref, g_ref, w_in_ref, qa_g_ref, qb_ref, kva_g_ref, kvb_ref, st_ref):
    q_lora = qa_g_ref.shape[1]
    kv_lora = kva_g_ref.shape[1]
    h = _rms_norm(x_ref[...], g_ref[...]).astype(BF16)
    proj = _dot(h, w_in_ref[...])
    q_lat = proj[:, POOL_W:POOL_W + q_lora]
    kv_lat = proj[:, POOL_W + q_lora:POOL_W + q_lora + kv_lora]
    kr_off = POOL_W + q_lora + kv_lora
    st_ref[:, ST_ZP:ST_KR] = proj[:, 0:POOL_W]
    st_ref[:, ST_KR:ST_Q] = proj[:, kr_off:kr_off + LANES]
    qn = _rms_norm(q_lat, qa_g_ref[...]).astype(BF16)
    st_ref[:, ST_Q:ST_KV] = _dot(qn, qb_ref[...])
    kvn = _rms_norm(kv_lat, kva_g_ref[...]).astype(BF16)
    st_ref[:, ST_KV:ST_WIDTH] = _dot(kvn, kvb_ref[...])


def _odd_pre_tail(st_ref, pos_ref, invf_ref, pool_w_ref, pool_scale_ref, qg_n_ref, qg_r_ref,
                  qg_sw_ref, kg_n_ref, kg_r_ref, kg_sw_ref, c_ref, q_ref, k_ref, v_ref,
                  bufs, seq_tile, tm):
    pool_w = POOL_W
    zbuf_ref, s2buf_ref, s4buf_ref, s8buf_ref = bufs
    zp = st_ref[:, ST_ZP:ST_KR]
    kr = st_ref[:, ST_KR:ST_Q]

    lo, hi = slice(0, LANES), slice(LANES, 2 * LANES)

    for buf in bufs:
        buf[0:POOL_CARRY_ROWS, :] = jnp.where(seq_tile == 0, 0.0, buf[0:POOL_CARRY_ROWS, :])

    body = slice(POOL_CARRY_ROWS, POOL_CARRY_ROWS + tm)
    zbuf_ref[body, :] = zp
    s2 = zp + _shifted(zbuf_ref, 1, tm, slice(0, pool_w))
    s2buf_ref[body, :] = s2
    s4 = s2 + _shifted(s2buf_ref, 2, tm, slice(0, pool_w))
    s4buf_ref[body, :] = s4[:, hi]
    s8 = s4[:, hi] + _shifted(s4buf_ref, 4, tm, lo)
    s8buf_ref[body, :] = s8
    s16 = s8 + _shifted(s8buf_ref, 8, tm, lo)
    for buf in bufs:
        buf[0:POOL_CARRY_ROWS, :] = buf[tm:tm + POOL_CARRY_ROWS, :]
    lane = lax.broadcasted_iota(jnp.int32, (tm, LANES), 1)
    first = lane < POOL_GROUP_DIM
    t1 = (seq_tile * tm + 1
          + lax.broadcasted_iota(jnp.int32, (tm, 1), 0)).astype(F32)
    inv = [1.0 / jnp.minimum(t1, float(w)) for w in POOL_WINDOWS]
    mean_lo = jnp.where(first, s2[:, lo] * inv[0], s4[:, lo] * inv[1])
    mean_hi = jnp.where(first, s8 * inv[2], s16 * inv[3])
    pooled = (jnp.concatenate([mean_lo, mean_hi], axis=1) - zp).astype(BF16)
    c_ref[...] = (_dot(pooled, pool_w_ref[...]) * pool_scale_ref[...]).astype(BF16)

    cosv, sinv = _rope_tables(pos_ref, invf_ref)

    def rope(pair, gain_cos, gain_sin):
        return pair * gain_cos + pltpu.roll(pair, QK_ROPE, 1) * gain_sin

    scale = QK_DIM ** -0.5 * math.log2(math.e)
    rope_w = MLA_HEADS * LANES
    q_gc, q_gs = qg_r_ref[...] * cosv, qg_sw_ref[...] * sinv
    kr_ss = 0.5 * jnp.sum(kr * kr, axis=-1, keepdims=True)
    kr_rot = rope(kr, kg_r_ref[...] * cosv, kg_sw_ref[...] * sinv)
    for hd in range(MLA_HEADS):
        q0 = ST_Q + hd * LANES
        q_n = st_ref[:, q0:q0 + LANES]
        q_pair = st_ref[:, q0 + rope_w:q0 + rope_w + LANES]
        ss = jnp.sum(q_n * q_n + 0.5 * (q_pair * q_pair), axis=-1, keepdims=True)
        rinv = lax.rsqrt(ss * (1.0 / QK_DIM) + EPS) * scale
        q_ref[hd, :, 0:QK_NOPE] = (q_n * qg_n_ref[...] * rinv).astype(BF16)
        q_ref[hd, :, QK_NOPE:] = (rope(q_pair, q_gc, q_gs) * rinv).astype(BF16)

        k0 = ST_KV + hd * (QK_NOPE + V_DIM)
        k_n = st_ref[:, k0:k0 + QK_NOPE]
        ss = jnp.sum(k_n * k_n, axis=-1, keepdims=True) + kr_ss
        rinv = lax.rsqrt(ss * (1.0 / QK_DIM) + EPS)
        k_ref[hd, :, 0:QK_NOPE] = (k_n * kg_n_ref[...] * rinv).astype(BF16)
        k_ref[hd, :, QK_NOPE:] = (kr_rot * rinv).astype(BF16)
        v_ref[hd] = st_ref[:, k0 + QK_NOPE:k0 + QK_NOPE + V_DIM].astype(BF16)


def _odd_pre_kernel(x_ref, pos_ref, invf_ref, g_ref, w_in_ref, pool_w_ref, pool_scale_ref,
                    qa_g_ref, qb_ref, kva_g_ref, kvb_ref, qg_n_ref, qg_r_ref, qg_sw_ref,
                    kg_n_ref, kg_r_ref, kg_sw_ref,
                    c_ref, q_ref, k_ref, v_ref, st_ref, zbuf_ref, s2buf_ref, s4buf_ref, s8buf_ref,
                    *, tm, seq_tiles):
    t = pl.program_id(0)
    seq_tile = lax.rem(jnp.maximum(t - 1, 0), seq_tiles)
    bufs = (zbuf_ref, s2buf_ref, s4buf_ref, s8buf_ref)

    @pl.when(t == 0)
    def _():
        st_ref[1] = jnp.zeros(st_ref.shape[1:], F32)
        for buf in bufs:
            buf[0:POOL_CARRY_ROWS, :] = jnp.zeros((POOL_CARRY_ROWS, buf.shape[1]), F32)

    def step(slot):
        _odd_pre_matmuls(x_ref, g_ref, w_in_ref, qa_g_ref, qb_ref, kva_g_ref, kvb_ref,
                         st_ref.at[slot])
        _odd_pre_tail(st_ref.at[1 - slot], pos_ref, invf_ref, pool_w_ref, pool_scale_ref,
                      qg_n_ref, qg_r_ref, qg_sw_ref, kg_n_ref, kg_r_ref, kg_sw_ref,
                      c_ref, q_ref, k_ref, v_ref, bufs, seq_tile, tm)

    for slot in range(2):
        pl.when(lax.rem(t, 2) == slot)(functools.partial(step, slot))


def _odd_pre(x, positions, g, w_in, pool_w, pool_scale, qa_g, qb, kva_g, kvb,
             qg_n, qg_r, qg_sw, kg_n, kg_r, kg_sw, *, tm):
    b, s, d = x.shape
    pool_width = pool_w.shape[0]
    qk_pad = QK_NOPE + LANES
    groups = LANES // (QK_ROPE // 2)
    seq_tiles = s // tm
    n_tiles = b * seq_tiles
    pos = positions.reshape(b, seq_tiles, groups, tm // groups).transpose(0, 1, 3, 2)
    inv_freq = ROPE_THETA ** (-jnp.arange(0, QK_ROPE, 2, dtype=F32) / QK_ROPE)
    invf = jnp.tile(inv_freq, groups).reshape(1, LANES)

    lead, trail = _skewed_tile_maps(n_tiles, seq_tiles)

    def head_map(t):
        i, j = trail(t)
        return i, 0, j, 0

    consts = [invf, g, w_in, pool_w, pool_scale, qa_g, qb, kva_g, kvb,
              qg_n, qg_r, qg_sw, kg_n, kg_r, kg_sw]
    head_spec = lambda w: pl.BlockSpec((None, MLA_HEADS, tm, w), head_map)
    carry = lambda w: pltpu.VMEM((tm + POOL_CARRY_ROWS, w), F32)
    return pl.pallas_call(
        functools.partial(_odd_pre_kernel, tm=tm, seq_tiles=seq_tiles),
        grid=(n_tiles + 1,),
        in_specs=[pl.BlockSpec((None, tm, d), lambda t: (*lead(t), 0)),
                  pl.BlockSpec((None, None, tm // groups, groups), lambda t: (*trail(t), 0, 0))]
                 + [_full_spec(c) for c in consts],
        out_specs=[pl.BlockSpec((None, tm, pool_width), lambda t: (*trail(t), 0)),
                   head_spec(qk_pad), head_spec(qk_pad), head_spec(V_DIM)],
        out_shape=[jax.ShapeDtypeStruct((b, s, pool_width), BF16),
                   jax.ShapeDtypeStruct((b, MLA_HEADS, s, qk_pad), BF16),
                   jax.ShapeDtypeStruct((b, MLA_HEADS, s, qk_pad), BF16),
                   jax.ShapeDtypeStruct((b, MLA_HEADS, s, V_DIM), BF16)],
        scratch_shapes=[pltpu.VMEM((2, tm, ST_WIDTH), F32),
                        carry(pool_width), carry(pool_width), carry(LANES), carry(LANES)],
        compiler_params=_params(1),
        name="odd_pre",
    )(x, pos, *consts)


def _attn_kernel(q_ref, k_ref, v_ref, o_ref, *, tq, hp):
    s_len = q_ref.shape[1]
    n_col = tq // LANES
    row = lax.broadcasted_iota(jnp.int32, (tq, tq), 0)
    col = lax.broadcasted_iota(jnp.int32, (tq, tq), 1)
    causal = row >= col

    def tile(hd, q0, k0, state):
        q = q_ref[hd, q0:q0 + tq, :]
        k = k_ref[hd, k0:k0 + tq, :]
        v = v_ref[hd, k0:k0 + tq, :]
        sc = lax.dot_general(q, k, (((1,), (1,)), ((), ())), preferred_element_type=F32)
        if k0 == q0:
            sc = jnp.where(causal, sc, MASK_VALUE)
        cols = [sc[:, c * LANES:(c + 1) * LANES] for c in range(n_col)]
        rmax = jnp.max(functools.reduce(jnp.maximum, cols), axis=-1, keepdims=True)
        if state is None:
            m_new = jnp.broadcast_to(rmax, (tq, LANES))
            ps = [jnp.exp2(c - m_new) for c in cols]
            return m_new, functools.reduce(jnp.add, ps), _dot(_bf16_cat(ps), v)
        m_prev, l_prev, acc_prev = state
        m_new = jnp.maximum(m_prev, rmax)
        alpha = jnp.exp2(m_prev - m_new)
        ps = [jnp.exp2(c - m_new) for c in cols]
        return (m_new, alpha * l_prev + functools.reduce(jnp.add, ps),
                alpha * acc_prev + _dot(_bf16_cat(ps), v))

    for qi in range(s_len // tq):
        for hd in range(hp):
            state = None
            for kj in range(qi + 1):
                state = tile(hd, qi * tq, kj * tq, state)
            _, l_part, acc = state
            l = jnp.sum(l_part, axis=-1, keepdims=True)
            o_ref[qi * tq:(qi + 1) * tq, hd * V_DIM:(hd + 1) * V_DIM] = (acc / l).astype(BF16)


def _bf16_cat(cols):
    return jnp.concatenate(cols, axis=1).astype(BF16)


def _attention(q, k, v, *, tq, hp):
    b, nh, s, qk_pad = q.shape
    qk_spec = pl.BlockSpec((None, hp, s, qk_pad), lambda i, j: (i, j, 0, 0))
    return pl.pallas_call(
        functools.partial(_attn_kernel, tq=tq, hp=hp),
        grid=(b, nh // hp),
        in_specs=[qk_spec, qk_spec,
                  pl.BlockSpec((None, hp, s, V_DIM), lambda i, j: (i, j, 0, 0))],
        out_specs=pl.BlockSpec((None, s, hp * V_DIM), lambda i, j: (i, 0, j)),
        out_shape=jax.ShapeDtypeStruct((b, s, nh * V_DIM), BF16),
        compiler_params=_params(),
        name="mla_attention",
    )(q, k, v)


def _pad_lanes(a, width):
    return jnp.pad(a, [(0, 0)] * (a.ndim - 1) + [(0, width - a.shape[-1])])


def _swap_halves(a, *, negate_first):
    a1, a2 = jnp.split(a, 2, axis=-1)
    return jnp.concatenate([-a2 if negate_first else a2, a1], axis=-1)


def _rope_pair(w):
    return jnp.concatenate([w, _swap_halves(w, negate_first=True)], axis=-1)


class _TileConfig(NamedTuple):
    tm_even: int
    tm_odd: int
    tm_ffn: int
    ff_chunk: int
    tq: int
    heads_per_step: int


def _tile_config(seq_len):
    cfg = _TileConfig(tm_even=1024, tm_odd=512, tm_ffn=1024, ff_chunk=2 * MXU_WIDTH,
                      tq=512, heads_per_step=3)
    assert all(seq_len % t == 0 for t in (cfg.tm_even, cfg.tm_odd, cfg.tm_ffn, cfg.tq))
    assert MLA_HEADS % cfg.heads_per_step == 0 and cfg.tm_even % SG_CHUNK == 0
    return cfg


def kernel(x, positions, mix_norm, ffn_norm, even_w_in, sg_ln_g, sg_w_s, sg_b_s, sc_conv_w,
           even_w_out, odd_w_in, pool_w, pool_scale, q_a_norm, q_b, kv_a_norm, kv_b, q_norm,
           k_norm, odd_w_out, ffn_w_gate, ffn_w_up, ffn_w_down):
    depth = mix_norm.shape[0]
    cfg = _tile_config(x.shape[1])
    row = lambda a: a.reshape(1, -1)

    for layer in range(depth):
        i = layer // 2
        if layer % 2 == 0:
            mix = _even_mixer(
                x, row(mix_norm[layer]), even_w_in[i].astype(BF16), row(sg_ln_g[i]),
                sg_w_s[i], sg_b_s[i].T, sc_conv_w[i], tm=cfg.tm_even)
            mixes, w_out = [mix], even_w_out
        else:
            pool_width = pool_scale.shape[-1]
            n_groups = pool_w.shape[1]
            gd = pool_w.shape[2]
            w_bd = jnp.zeros((pool_width, pool_width), F32)
            for gidx in range(n_groups):
                w_bd = w_bd.at[gidx * gd:(gidx + 1) * gd, gidx * gd:(gidx + 1) * gd].set(pool_w[i, gidx])
            q_lora = q_b.shape[1]
            qb3 = q_b[i].reshape(q_lora, MLA_HEADS, QK_DIM)
            qb_nope = qb3[:, :, :QK_NOPE].reshape(q_lora, MLA_HEADS * QK_NOPE)
            qb_pair = _rope_pair(qb3[:, :, QK_NOPE:]).reshape(q_lora, MLA_HEADS * LANES)
            qb = jnp.concatenate([qb_nope, qb_pair], axis=1).astype(BF16)
            w_in = jnp.concatenate(
                [odd_w_in[i][:, :-QK_ROPE], _rope_pair(odd_w_in[i][:, -QK_ROPE:])],
                axis=1).astype(BF16)
            gain_rows = []
            for gn in (q_norm[i], k_norm[i]):
                g_rope = row(gn[QK_NOPE:])
                gain_rows += [row(gn[:QK_NOPE]), _pad_lanes(g_rope, LANES),
                              _pad_lanes(_swap_halves(g_rope, negate_first=False), LANES)]
            c_out, q, k, v = _odd_pre(
                x, positions, row(mix_norm[layer]), w_in, w_bd.astype(BF16), row(pool_scale[i]),
                row(q_a_norm[i]), qb, row(kv_a_norm[i]), kv_b[i].astype(BF16), *gain_rows,
                tm=cfg.tm_odd)
            d_out = _attention(q, k, v, tq=cfg.tq, hp=cfg.heads_per_step)
            mixes, w_out = [c_out, d_out], odd_w_out
        x = _post(x, mixes, w_out, i, row(ffn_norm[layer]), ffn_w_gate, ffn_w_up, ffn_w_down,
                  layer, tm=cfg.tm_ffn, ff_chunk=cfg.ff_chunk)
    return x
```

```python
import functools
import math
from typing import NamedTuple

import jax
import jax.numpy as jnp
from jax import lax
from jax.experimental import pallas as pl
from jax.experimental.pallas import tpu as pltpu

F32 = jnp.float32
BF16 = jnp.bfloat16

EPS = 1e-6
MASK_VALUE = -1e30
LANES = 128
MXU_WIDTH = 256
SG_CHUNK = 128
SG_HEADS = 4
CONV_WIDTH = 3
POOL_WINDOWS = (2, 4, 8, 16)
POOL_GROUP_DIM = 64
MLA_HEADS = 6
QK_NOPE = 128
QK_ROPE = 64
QK_DIM = QK_NOPE + QK_ROPE
V_DIM = 128
ROPE_THETA = 10000.0
CARRY_ROWS = 8
POOL_CARRY_ROWS = 16
VMEM_LIMIT_BYTES = 58 * 1024 * 1024


def _rms_norm(x, g):
    ms = jnp.mean(x * x, axis=-1, keepdims=True)
    return x * lax.rsqrt(ms + EPS) * g


def _gelu(x):
    return 0.5 * x * (1.0 + lax.erf(x * math.sqrt(0.5)))


def _dot(a, b):
    return jnp.dot(a, b, preferred_element_type=F32)


def _full_spec(arr):
    nd = arr.ndim
    return pl.BlockSpec(arr.shape, lambda *_: (0,) * nd)


def _params(grid_rank=2):
    return pltpu.CompilerParams(
        dimension_semantics=("arbitrary",) * grid_rank,
        vmem_limit_bytes=VMEM_LIMIT_BYTES)


def _even_mixer_kernel(x_ref, g_ref, w_in_ref, ln_g_ref, w_s_ref, b_st_ref,
                       conv_w_ref, mix_ref, zbuf_ref, *, tm):
    sg_w = SG_HEADS * LANES
    h = _rms_norm(x_ref[...], g_ref[...]).astype(BF16)
    proj = _dot(h, w_in_ref[...])
    u = _gelu(proj[:, 0:sg_w])
    v = _gelu(proj[:, sg_w:2 * sg_w])
    sc_w = (proj.shape[1] - 2 * sg_w) // 3
    b_gate = proj[:, 2 * sg_w:2 * sg_w + sc_w]
    c_gate = proj[:, 2 * sg_w + sc_w:2 * sg_w + 2 * sc_w]
    hv = proj[:, 2 * sg_w + 2 * sc_w:]

    row = lax.broadcasted_iota(jnp.int32, (SG_CHUNK, SG_CHUNK), 0)
    col = lax.broadcasted_iota(jnp.int32, (SG_CHUNK, SG_CHUNK), 1)
    causal = row >= col
    for hd in range(SG_HEADS):
        cs = slice(hd * LANES, (hd + 1) * LANES)
        vh = v[:, cs]
        mu = jnp.mean(vh, axis=-1, keepdims=True)
        xc = vh - mu
        var = jnp.mean(xc * xc, axis=-1, keepdims=True)
        vn = (xc * lax.rsqrt(var + EPS) * ln_g_ref[:, cs]).astype(BF16)
        w = jnp.where(causal, w_s_ref[hd], 0.0).astype(BF16)
        bias = b_st_ref[:, hd:hd + 1]
        for c in range(tm // SG_CHUNK):
            rs = slice(c * SG_CHUNK, (c + 1) * SG_CHUNK)
            mixed = _dot(w, vn[rs]) + bias
            mix_ref[rs, cs] = (u[rs, cs] * mixed).astype(BF16)

    @pl.when(pl.program_id(1) == 0)
    def _():
        zbuf_ref[0:CARRY_ROWS, :] = jnp.zeros((CARRY_ROWS, sc_w), F32)

    z = c_gate * hv
    zbuf_ref[CARRY_ROWS:CARRY_ROWS + tm, :] = z
    y = conv_w_ref[CONV_WIDTH - 1:CONV_WIDTH, :] * z
    for k in range(CONV_WIDTH - 1):
        shift = CONV_WIDTH - 1 - k
        y = y + conv_w_ref[k:k + 1, :] * zbuf_ref[CARRY_ROWS - shift:CARRY_ROWS - shift + tm, :]
    mix_ref[:, sg_w:] = (b_gate * y).astype(BF16)
    zbuf_ref[0:CARRY_ROWS, :] = zbuf_ref[tm:tm + CARRY_ROWS, :]


def _skewed_tile_maps(n_tiles, seq_tiles):
    def lead(t):
        tt = jnp.minimum(t, n_tiles - 1)
        return tt // seq_tiles, lax.rem(tt, seq_tiles)

    def trail(t):
        tt = jnp.maximum(t - 1, 0)
        return tt // seq_tiles, lax.rem(tt, seq_tiles)

    return lead, trail


def _even_mixer(x, g, w_in, ln_g, w_s, b_st, conv_w, *, tm):
    b, s, d = x.shape
    sc_w = conv_w.shape[1]
    mix_w = SG_HEADS * LANES + sc_w
    tok = pl.BlockSpec((None, tm, d), lambda i, j: (i, j, 0))
    return pl.pallas_call(
        functools.partial(_even_mixer_kernel, tm=tm),
        grid=(b, s // tm),
        in_specs=[tok, _full_spec(g), _full_spec(w_in), _full_spec(ln_g),
                  _full_spec(w_s), _full_spec(b_st), _full_spec(conv_w)],
        out_specs=pl.BlockSpec((None, tm, mix_w), lambda i, j: (i, j, 0)),
        out_shape=jax.ShapeDtypeStruct((b, s, mix_w), BF16),
        scratch_shapes=[pltpu.VMEM((tm + CARRY_ROWS, sc_w), F32)],
        compiler_params=_params(),
        name="even_mixer",
    )(x, g, w_in, ln_g, w_s, b_st, conv_w)


W_CAST_STEPS = 16


def _post_kernel(*refs, n_mix, ff_chunk):
    x_ref = refs[0]
    mix_refs = refs[1:1 + n_mix]
    (wo_ref, g_ref, wg_ref, wu_ref, wd_ref, o_ref,
     wo_s, wg_s, wu_s, wd_s) = refs[1 + n_mix:]
    step = pl.program_id(0)

    @pl.when(step < W_CAST_STEPS)
    def _():
        for src, dst in ((wo_ref, wo_s), (wg_ref, wg_s), (wu_ref, wu_s), (wd_ref, wd_s)):
            rows = src.shape[0]
            dst[pl.ds(pl.multiple_of(step * rows, rows), rows), :] = src[...].astype(BF16)

    @pl.when(step >= W_CAST_STEPS)
    def _():
        mix = [m_ref[...] for m_ref in mix_refs]
        mix = mix[0] if n_mix == 1 else jnp.concatenate(mix, axis=1)
        x1 = x_ref[...] + _dot(mix, wo_s[...])
        h = _rms_norm(x1, g_ref[...]).astype(BF16)
        acc = x1
        d_ff = wg_s.shape[1]
        for c0 in range(0, d_ff, ff_chunk):
            cs = slice(c0, min(c0 + ff_chunk, d_ff))
            gate = _dot(h, wg_s[:, cs])
            up = _dot(h, wu_s[:, cs])
            act = (gate / (1.0 + jnp.exp(-gate)) * up).astype(BF16)
            acc = acc + _dot(act, wd_s[cs, :])
        o_ref[...] = acc


def _post(x, mixes, w_out, out_layer, g, wg, wu, wd, layer, *, tm, ff_chunk):
    b, s, d = x.shape
    seq_tiles = s // tm

    def tile(t):
        tt = jnp.maximum(t - W_CAST_STEPS, 0)
        return tt // seq_tiles, lax.rem(tt, seq_tiles)

    def chunk_spec(w, lyr):
        rows = w.shape[1] // W_CAST_STEPS
        return pl.BlockSpec((None, rows, w.shape[2]),
                            lambda t: (lyr, jnp.minimum(t, W_CAST_STEPS - 1), 0))

    tok = pl.BlockSpec((None, tm, d), lambda t: (*tile(t), 0))
    mix_specs = [pl.BlockSpec((None, tm, m.shape[2]), lambda t: (*tile(t), 0)) for m in mixes]
    return pl.pallas_call(
        functools.partial(_post_kernel, n_mix=len(mixes), ff_chunk=ff_chunk),
        grid=(W_CAST_STEPS + b * seq_tiles,),
        in_specs=[tok] + mix_specs
                 + [chunk_spec(w_out, out_layer), _full_spec(g), chunk_spec(wg, layer),
                    chunk_spec(wu, layer), chunk_spec(wd, layer)],
        out_specs=tok,
        out_shape=jax.ShapeDtypeStruct(x.shape, x.dtype),
        scratch_shapes=[pltpu.VMEM(w.shape[1:], BF16) for w in (w_out, wg, wu, wd)],
        compiler_params=_params(1),
        name="outproj_ffn",
    )(x, *mixes, w_out, g, wg, wu, wd)


def _rope_tables(pos_ref, invf_ref):
    half = QK_ROPE // 2
    groups = LANES // half
    rows = pos_ref.shape[0]
    pos = pos_ref[...].astype(F32)
    lane = lax.broadcasted_iota(jnp.int32, (rows, LANES), 1)
    p = pos[:, 0:1]
    for grp in range(1, groups):
        p = jnp.where(lane >= grp * half, pos[:, grp:grp + 1], p)
    ang = p * invf_ref[...]
    tables = []
    for packed in (jnp.cos(ang), jnp.sin(ang)):
        quarters = []
        for grp in range(groups):
            r = packed if grp == 0 else pltpu.roll(packed, LANES - grp * half, 1)
            quarters.append(jnp.where(lane < half, r, pltpu.roll(r, half, 1)))
        tables.append(jnp.concatenate(quarters, axis=0))
    return tables


def _shifted(buf_ref, shift, tm, cs):
    return buf_ref[POOL_CARRY_ROWS - shift:POOL_CARRY_ROWS - shift + tm, cs]


POOL_W = len(POOL_WINDOWS) * POOL_GROUP_DIM
ST_ZP = 0
ST_KR = ST_ZP + POOL_W
ST_Q = ST_KR + LANES
ST_KV = ST_Q + 2 * MLA_HEADS * LANES
ST_WIDTH = ST_KV + MLA_HEADS * (QK_NOPE + V_DIM)


def _odd_pre_matmuls(x_ref, g_ref, w_in_ref, qa_g_ref, qb_ref, kva_g_ref, kvb_ref, st_ref):
    q_lora = qa_g_ref.shape[1]
    kv_lora = kva_g_ref.shape[1]
    h = _rms_norm(x_ref[...], g_ref[...]).astype(BF16)
    proj = _dot(h, w_in_ref[...])
    q_lat = proj[:, POOL_W:POOL_W + q_lora]
    kv_lat = proj[:, POOL_W + q_lora:POOL_W + q_lora + kv_lora]
    kr_off = POOL_W + q_lora + kv_lora
    st_ref[:, ST_ZP:ST_KR] = proj[:, 0:POOL_W]
    st_ref[:, ST_KR:ST_Q] = proj[:, kr_off:kr_off + LANES]
    qn = _rms_norm(q_lat, qa_g_ref[...]).astype(BF16)
    st_ref[:, ST_Q:ST_KV] = _dot(qn, qb_ref[...])
    kvn = _rms_norm(kv_lat, kva_g_ref[...]).astype(BF16)
    st_ref[:, ST_KV:ST_WIDTH] = _dot(kvn, kvb_ref[...])


def _odd_pre_tail(st_ref, pos_ref, invf_ref, pool_w_ref, pool_scale_ref, qg_n_ref, qg_r_ref,
                  qg_sw_ref, kg_n_ref, kg_r_ref, kg_sw_ref, c_ref, q_ref, k_ref, v_ref,
                  bufs, seq_tile, tm):
    pool_w = POOL_W
    zbuf_ref, s2buf_ref, s4buf_ref, s8buf_ref = bufs
    zp = st_ref[:, ST_ZP:ST_KR]
    kr = st_ref[:, ST_KR:ST_Q]

    lo, hi = slice(0, LANES), slice(LANES, 2 * LANES)

    for buf in bufs:
        buf[0:POOL_CARRY_ROWS, :] = jnp.where(seq_tile == 0, 0.0, buf[0:POOL_CARRY_ROWS, :])

    body = slice(POOL_CARRY_ROWS, POOL_CARRY_ROWS + tm)
    zbuf_ref[body, :] = zp
    s2 = zp + _shifted(zbuf_ref, 1, tm, slice(0, pool_w))
    s2buf_ref[body, :] = s2
    s4 = s2 + _shifted(s2buf_ref, 2, tm, slice(0, pool_w))
    s4buf_ref[body, :] = s4[:, hi]
    s8 = s4[:, hi] + _shifted(s4buf_ref, 4, tm, lo)
    s8buf_ref[body, :] = s8
    s16 = s8 + _shifted(s8buf_ref, 8, tm, lo)
    for buf in bufs:
        buf[0:POOL_CARRY_ROWS, :] = buf[tm:tm + POOL_CARRY_ROWS, :]
    lane = lax.broadcasted_iota(jnp.int32, (tm, LANES), 1)
    first = lane < POOL_GROUP_DIM
    t1 = (seq_tile * tm + 1
          + lax.broadcasted_iota(jnp.int32, (tm, 1), 0)).astype(F32)
    inv = [1.0 / jnp.minimum(t1, float(w)) for w in POOL_WINDOWS]
    mean_lo = jnp.where(first, s2[:, lo] * inv[0], s4[:, lo] * inv[1])
    mean_hi = jnp.where(first, s8 * inv[2], s16 * inv[3])
    pooled = (jnp.concatenate([mean_lo, mean_hi], axis=1) - zp).astype(BF16)
    c_ref[...] = (_dot(pooled, pool_w_ref[...]) * pool_scale_ref[...]).astype(BF16)

    cosv, sinv = _rope_tables(pos_ref, invf_ref)

    def rope(pair, gain_cos, gain_sin):
        return pair * gain_cos + pltpu.roll(pair, QK_ROPE, 1) * gain_sin

    scale = QK_DIM ** -0.5 * math.log2(math.e)
    rope_w = MLA_HEADS * LANES
    q_gc, q_gs = qg_r_ref[...] * cosv, qg_sw_ref[...] * sinv
    kr_ss = 0.5 * jnp.sum(kr * kr, axis=-1, keepdims=True)
    kr_rot = rope(kr, kg_r_ref[...] * cosv, kg_sw_ref[...] * sinv)
    for hd in range(MLA_HEADS):
        q0 = ST_Q + hd * LANES
        q_n = st_ref[:, q0:q0 + LANES]
        q_pair = st_ref[:, q0 + rope_w:q0 + rope_w + LANES]
        ss = jnp.sum(q_n * q_n + 0.5 * (q_pair * q_pair), axis=-1, keepdims=True)
        rinv = lax.rsqrt(ss * (1.0 / QK_DIM) + EPS) * scale
        q_ref[hd, :, 0:QK_NOPE] = (q_n * qg_n_ref[...] * rinv).astype(BF16)
        q_ref[hd, :, QK_NOPE:] = (rope(q_pair, q_gc, q_gs) * rinv).astype(BF16)

        k0 = ST_KV + hd * (QK_NOPE + V_DIM)
        k_n = st_ref[:, k0:k0 + QK_NOPE]
        ss = jnp.sum(k_n * k_n, axis=-1, keepdims=True) + kr_ss
        rinv = lax.rsqrt(ss * (1.0 / QK_DIM) + EPS)
        k_ref[hd, :, 0:QK_NOPE] = (k_n * kg_n_ref[...] * rinv).astype(BF16)
        k_ref[hd, :, QK_NOPE:] = (kr_rot * rinv).astype(BF16)
        v_ref[hd] = st_ref[:, k0 + QK_NOPE:k0 + QK_NOPE + V_DIM].astype(BF16)


def _odd_pre_kernel(x_ref, pos_ref, invf_ref, g_ref, w_in_ref, pool_w_ref, pool_scale_ref,
                    qa_g_ref, qb_ref, kva_g_ref, kvb_ref, qg_n_ref, qg_r_ref, qg_sw_ref,
                    kg_n_ref, kg_r_ref, kg_sw_ref,
                    c_ref, q_ref, k_ref, v_ref, st_ref, zbuf_ref, s2buf_ref, s4buf_ref, s8buf_ref,
                    *, tm, seq_tiles):
    t = pl.program_id(0)
    seq_tile = lax.rem(jnp.maximum(t - 1, 0), seq_tiles)
    bufs = (zbuf_ref, s2buf_ref, s4buf_ref, s8buf_ref)

    @pl.when(t == 0)
    def _():
        st_ref[1] = jnp.zeros(st_ref.shape[1:], F32)
        for buf in bufs:
            buf[0:POOL_CARRY_ROWS, :] = jnp.zeros((POOL_CARRY_ROWS, buf.shape[1]), F32)

    def step(slot):
        _odd_pre_matmuls(x_ref, g_ref, w_in_ref, qa_g_ref, qb_ref, kva_g_ref, kvb_ref,
                         st_ref.at[slot])
        _odd_pre_tail(st_ref.at[1 - slot], pos_ref, invf_ref, pool_w_ref, pool_scale_ref,
                      qg_n_ref, qg_r_ref, qg_sw_ref, kg_n_ref, kg_r_ref, kg_sw_ref,
                      c_ref, q_ref, k_ref, v_ref, bufs, seq_tile, tm)

    for slot in range(2):
        pl.when(lax.rem(t, 2) == slot)(functools.partial(step, slot))


def _odd_pre(x, positions, g, w_in, pool_w, pool_scale, qa_g, qb, kva_g, kvb,
             qg_n, qg_r, qg_sw, kg_n, kg_r, kg_sw, *, tm):
    b, s, d = x.shape
    pool_width = pool_w.shape[0]
    qk_pad = QK_NOPE + LANES
    groups = LANES // (QK_ROPE // 2)
    seq_tiles = s // tm
    n_tiles = b * seq_tiles
    pos = positions.reshape(b, seq_tiles, groups, tm // groups).transpose(0, 1, 3, 2)
    inv_freq = ROPE_THETA ** (-jnp.arange(0, QK_ROPE, 2, dtype=F32) / QK_ROPE)
    invf = jnp.tile(inv_freq, groups).reshape(1, LANES)

    lead, trail = _skewed_tile_maps(n_tiles, seq_tiles)

    def head_map(t):
        i, j = trail(t)
        return i, 0, j, 0

    consts = [invf, g, w_in, pool_w, pool_scale, qa_g, qb, kva_g, kvb,
              qg_n, qg_r, qg_sw, kg_n, kg_r, kg_sw]
    head_spec = lambda w: pl.BlockSpec((None, MLA_HEADS, tm, w), head_map)
    carry = lambda w: pltpu.VMEM((tm + POOL_CARRY_ROWS, w), F32)
    return pl.pallas_call(
        functools.partial(_odd_pre_kernel, tm=tm, seq_tiles=seq_tiles),
        grid=(n_tiles + 1,),
        in_specs=[pl.BlockSpec((None, tm, d), lambda t: (*lead(t), 0)),
                  pl.BlockSpec((None, None, tm // groups, groups), lambda t: (*trail(t), 0, 0))]
                 + [_full_spec(c) for c in consts],
        out_specs=[pl.BlockSpec((None, tm, pool_width), lambda t: (*trail(t), 0)),
                   head_spec(qk_pad), head_spec(qk_pad), head_spec(V_DIM)],
        out_shape=[jax.ShapeDtypeStruct((b, s, pool_width), BF16),
                   jax.ShapeDtypeStruct((b, MLA_HEADS, s, qk_pad), BF16),
                   jax.ShapeDtypeStruct((b, MLA_HEADS, s, qk_pad), BF16),
                   jax.ShapeDtypeStruct((b, MLA_HEADS, s, V_DIM), BF16)],
        scratch_shapes=[pltpu.VMEM((2, tm, ST_WIDTH), F32),
                        carry(pool_width), carry(pool_width), carry(LANES), carry(LANES)],
        compiler_params=_params(1),
        name="odd_pre",
    )(x, pos, *consts)


def _attn_kernel(q_ref, k_ref, v_ref, o_ref, *, tq, hp):
    s_len = q_ref.shape[1]
    n_col = tq // LANES
    row = lax.broadcasted_iota(jnp.int32, (tq, tq), 0)
    col = lax.broadcasted_iota(jnp.int32, (tq, tq), 1)
    causal = row >= col

    def tile(hd, q0, k0, state):
        q = q_ref[hd, q0:q0 + tq, :]
        k = k_ref[hd, k0:k0 + tq, :]
        v = v_ref[hd, k0:k0 + tq, :]
        sc = lax.dot_general(q, k, (((1,), (1,)), ((), ())), preferred_element_type=F32)
        if k0 == q0:
            sc = jnp.where(causal, sc, MASK_VALUE)
        cols = [sc[:, c * LANES:(c + 1) * LANES] for c in range(n_col)]
        rmax = jnp.max(functools.reduce(jnp.maximum, cols), axis=-1, keepdims=True)
        if state is None:
            m_new = jnp.broadcast_to(rmax, (tq, LANES))
            ps = [jnp.exp2(c - m_new) for c in cols]
            return m_new, functools.reduce(jnp.add, ps), _dot(_bf16_cat(ps), v)
        m_prev, l_prev, acc_prev = state
        m_new = jnp.maximum(m_prev, rmax)
        alpha = jnp.exp2(m_prev - m_new)
        ps = [jnp.exp2(c - m_new) for c in cols]
        return (m_new, alpha * l_prev + functools.reduce(jnp.add, ps),
                alpha * acc_prev + _dot(_bf16_cat(ps), v))

    for qi in range(s_len // tq):
        for hd in range(hp):
            state = None
            for kj in range(qi + 1):
                state = tile(hd, qi * tq, kj * tq, state)
            _, l_part, acc = state
            l = jnp.sum(l_part, axis=-1, keepdims=True)
            o_ref[qi * tq:(qi + 1) * tq, hd * V_DIM:(hd + 1) * V_DIM] = (acc / l).astype(BF16)


def _bf16_cat(cols):
    return jnp.concatenate(cols, axis=1).astype(BF16)


def _attention(q, k, v, *, tq, hp):
    b, nh, s, qk_pad = q.shape
    qk_spec = pl.BlockSpec((None, hp, s, qk_pad), lambda i, j: (i, j, 0, 0))
    return pl.pallas_call(
        functools.partial(_attn_kernel, tq=tq, hp=hp),
        grid=(b, nh // hp),
        in_specs=[qk_spec, qk_spec,
                  pl.BlockSpec((None, hp, s, V_DIM), lambda i, j: (i, j, 0, 0))],
        out_specs=pl.BlockSpec((None, s, hp * V_DIM), lambda i, j: (i, 0, j)),
        out_shape=jax.ShapeDtypeStruct((b, s, nh * V_DIM), BF16),
        compiler_params=_params(),
        name="mla_attention",
    )(q, k, v)


def _pad_lanes(a, width):
    return jnp.pad(a, [(0, 0)] * (a.ndim - 1) + [(0, width - a.shape[-1])])


def _swap_halves(a, *, negate_first):
    a1, a2 = jnp.split(a, 2, axis=-1)
    return jnp.concatenate([-a2 if negate_first else a2, a1], axis=-1)


def _rope_pair(w):
    return jnp.concatenate([w, _swap_halves(w, negate_first=True)], axis=-1)


class _TileConfig(NamedTuple):
    tm_even: int
    tm_odd: int
    tm_ffn: int
    ff_chunk: int
    tq: int
    heads_per_step: int


def _tile_config(seq_len):
    cfg = _TileConfig(tm_even=1024, tm_odd=512, tm_ffn=1024, ff_chunk=2 * MXU_WIDTH,
                      tq=256, heads_per_step=2)
    assert all(seq_len % t == 0 for t in (cfg.tm_even, cfg.tm_odd, cfg.tm_ffn, cfg.tq))
    assert MLA_HEADS % cfg.heads_per_step == 0 and cfg.tm_even % SG_CHUNK == 0
    return cfg


def kernel(x, positions, mix_norm, ffn_norm, even_w_in, sg_ln_g, sg_w_s, sg_b_s, sc_conv_w,
           even_w_out, odd_w_in, pool_w, pool_scale, q_a_norm, q_b, kv_a_norm, kv_b, q_norm,
           k_norm, odd_w_out, ffn_w_gate, ffn_w_up, ffn_w_down):
    depth = mix_norm.shape[0]
    cfg = _tile_config(x.shape[1])
    row = lambda a: a.reshape(1, -1)

    for layer in range(depth):
        i = layer // 2
        if layer % 2 == 0:
            mix = _even_mixer(
                x, row(mix_norm[layer]), even_w_in[i].astype(BF16), row(sg_ln_g[i]),
                sg_w_s[i], sg_b_s[i].T, sc_conv_w[i], tm=cfg.tm_even)
            mixes, w_out = [mix], even_w_out
        else:
            pool_width = pool_scale.shape[-1]
            n_groups = pool_w.shape[1]
            gd = pool_w.shape[2]
            w_bd = jnp.zeros((pool_width, pool_width), F32)
            for gidx in range(n_groups):
                w_bd = w_bd.at[gidx * gd:(gidx + 1) * gd, gidx * gd:(gidx + 1) * gd].set(pool_w[i, gidx])
            q_lora = q_b.shape[1]
            qb3 = q_b[i].reshape(q_lora, MLA_HEADS, QK_DIM)
            qb_nope = qb3[:, :, :QK_NOPE].reshape(q_lora, MLA_HEADS * QK_NOPE)
            qb_pair = _rope_pair(qb3[:, :, QK_NOPE:]).reshape(q_lora, MLA_HEADS * LANES)
            qb = jnp.concatenate([qb_nope, qb_pair], axis=1).astype(BF16)
            w_in = jnp.concatenate(
                [odd_w_in[i][:, :-QK_ROPE], _rope_pair(odd_w_in[i][:, -QK_ROPE:])],
                axis=1).astype(BF16)
            gain_rows = []
            for gn in (q_norm[i], k_norm[i]):
                g_rope = row(gn[QK_NOPE:])
                gain_rows += [row(gn[:QK_NOPE]), _pad_lanes(g_rope, LANES),
                              _pad_lanes(_swap_halves(g_rope, negate_first=False), LANES)]
            c_out, q, k, v = _odd_pre(
                x, positions, row(mix_norm[layer]), w_in, w_bd.astype(BF16), row(pool_scale[i]),
                row(q_a_norm[i]), qb, row(kv_a_norm[i]), kv_b[i].astype(BF16), *gain_rows,
                tm=cfg.tm_odd)
            d_out = _attention(q, k, v, tq=cfg.tq, hp=cfg.heads_per_step)
            mixes, w_out = [c_out, d_out], odd_w_out
        x = _post(x, mixes, w_out, i, row(ffn_norm[layer]), ffn_w_gate, ffn_w_up, ffn_w_down,
                  layer, tm=cfg.tm_ffn, ff_chunk=cfg.ff_chunk)
    return x
```

```python
import functools
import math
from typing import NamedTuple

import jax
import jax.numpy as jnp
from jax import lax
from jax.experimental import pallas as pl
from jax.experimental.pallas import tpu as pltpu

F32 = jnp.float32
BF16 = jnp.bfloat16

EPS = 1e-6
MASK_VALUE = -1e30
LANES = 128
MXU_WIDTH = 256
SG_CHUNK = 128
SG_HEADS = 4
CONV_WIDTH = 3
POOL_WINDOWS = (2, 4, 8, 16)
POOL_GROUP_DIM = 64
MLA_HEADS = 6
QK_NOPE = 128
QK_ROPE = 64
QK_DIM = QK_NOPE + QK_ROPE
V_DIM = 128
ROPE_THETA = 10000.0
CARRY_ROWS = 8
POOL_CARRY_ROWS = 16
VMEM_LIMIT_BYTES = 58 * 1024 * 1024


def _rms_norm(x, g):
    ms = jnp.mean(x * x, axis=-1, keepdims=True)
    return x * lax.rsqrt(ms + EPS) * g


def _gelu(x):
    return 0.5 * x * (1.0 + lax.erf(x * math.sqrt(0.5)))


def _dot(a, b):
    return jnp.dot(a, b, preferred_element_type=F32)


def _full_spec(arr):
    nd = arr.ndim
    return pl.BlockSpec(arr.shape, lambda *_: (0,) * nd)


def _params(grid_rank=2):
    return pltpu.CompilerParams(
        dimension_semantics=("arbitrary",) * grid_rank,
        vmem_limit_bytes=VMEM_LIMIT_BYTES)


def _even_mixer_kernel(x_ref, g_ref, w_in_ref, ln_g_ref, w_s_ref, b_st_ref,
                       conv_w_ref, mix_ref, zbuf_ref, *, tm):
    sg_w = SG_HEADS * LANES
    h = _rms_norm(x_ref[...], g_ref[...]).astype(BF16)
    proj = _dot(h, w_in_ref[...])
    u = _gelu(proj[:, 0:sg_w])
    v = _gelu(proj[:, sg_w:2 * sg_w])
    sc_w = (proj.shape[1] - 2 * sg_w) // 3
    b_gate = proj[:, 2 * sg_w:2 * sg_w + sc_w]
    c_gate = proj[:, 2 * sg_w + sc_w:2 * sg_w + 2 * sc_w]
    hv = proj[:, 2 * sg_w + 2 * sc_w:]

    row = lax.broadcasted_iota(jnp.int32, (SG_CHUNK, SG_CHUNK), 0)
    col = lax.broadcasted_iota(jnp.int32, (SG_CHUNK, SG_CHUNK), 1)
    causal = row >= col
    for hd in range(SG_HEADS):
        cs = slice(hd * LANES, (hd + 1) * LANES)
        vh = v[:, cs]
        mu = jnp.mean(vh, axis=-1, keepdims=True)
        xc = vh - mu
        var = jnp.mean(xc * xc, axis=-1, keepdims=True)
        vn = (xc * lax.rsqrt(var + EPS) * ln_g_ref[:, cs]).astype(BF16)
        w = jnp.where(causal, w_s_ref[hd], 0.0).astype(BF16)
        bias = b_st_ref[:, hd:hd + 1]
        for c in range(tm // SG_CHUNK):
            rs = slice(c * SG_CHUNK, (c + 1) * SG_CHUNK)
            mixed = _dot(w, vn[rs]) + bias
            mix_ref[rs, cs] = (u[rs, cs] * mixed).astype(BF16)

    @pl.when(pl.program_id(1) == 0)
    def _():
        zbuf_ref[0:CARRY_ROWS, :] = jnp.zeros((CARRY_ROWS, sc_w), F32)

    z = c_gate * hv
    zbuf_ref[CARRY_ROWS:CARRY_ROWS + tm, :] = z
    y = conv_w_ref[CONV_WIDTH - 1:CONV_WIDTH, :] * z
    for k in range(CONV_WIDTH - 1):
        shift = CONV_WIDTH - 1 - k
        y = y + conv_w_ref[k:k + 1, :] * zbuf_ref[CARRY_ROWS - shift:CARRY_ROWS - shift + tm, :]
    mix_ref[:, sg_w:] = (b_gate * y).astype(BF16)
    zbuf_ref[0:CARRY_ROWS, :] = zbuf_ref[tm:tm + CARRY_ROWS, :]


def _skewed_tile_maps(n_tiles, seq_tiles):
    def lead(t):
        tt = jnp.minimum(t, n_tiles - 1)
        return tt // seq_tiles, lax.rem(tt, seq_tiles)

    def trail(t):
        tt = jnp.maximum(t - 1, 0)
        return tt // seq_tiles, lax.rem(tt, seq_tiles)

    return lead, trail


def _even_mixer(x, g, w_in, ln_g, w_s, b_st, conv_w, *, tm):
    b, s, d = x.shape
    sc_w = conv_w.shape[1]
    mix_w = SG_HEADS * LANES + sc_w
    tok = pl.BlockSpec((None, tm, d), lambda i, j: (i, j, 0))
    return pl.pallas_call(
        functools.partial(_even_mixer_kernel, tm=tm),
        grid=(b, s // tm),
        in_specs=[tok, _full_spec(g), _full_spec(w_in), _full_spec(ln_g),
                  _full_spec(w_s), _full_spec(b_st), _full_spec(conv_w)],
        out_specs=pl.BlockSpec((None, tm, mix_w), lambda i, j: (i, j, 0)),
        out_shape=jax.ShapeDtypeStruct((b, s, mix_w), BF16),
        scratch_shapes=[pltpu.VMEM((tm + CARRY_ROWS, sc_w), F32)],
        compiler_params=_params(),
        name="even_mixer",
    )(x, g, w_in, ln_g, w_s, b_st, conv_w)


W_CAST_STEPS = 16


def _post_kernel(*refs, n_mix, ff_chunk):
    x_ref = refs[0]
    mix_refs = refs[1:1 + n_mix]
    (wo_ref, g_ref, wg_ref, wu_ref, wd_ref, o_ref,
     wo_s, wg_s, wu_s, wd_s) = refs[1 + n_mix:]
    step = pl.program_id(0)

    @pl.when(step < W_CAST_STEPS)
    def _():
        for src, dst in ((wo_ref, wo_s), (wg_ref, wg_s), (wu_ref, wu_s), (wd_ref, wd_s)):
            rows = src.shape[0]
            dst[pl.ds(pl.multiple_of(step * rows, rows), rows), :] = src[...].astype(BF16)

    @pl.when(step >= W_CAST_STEPS)
    def _():
        mix = [m_ref[...] for m_ref in mix_refs]
        mix = mix[0] if n_mix == 1 else jnp.concatenate(mix, axis=1)
        x1 = x_ref[...] + _dot(mix, wo_s[...])
        h = _rms_norm(x1, g_ref[...]).astype(BF16)
        acc = x1
        d_ff = wg_s.shape[1]
        for c0 in range(0, d_ff, ff_chunk):
            cs = slice(c0, min(c0 + ff_chunk, d_ff))
            gate = _dot(h, wg_s[:, cs])
            up = _dot(h, wu_s[:, cs])
            act = (gate / (1.0 + jnp.exp(-gate)) * up).astype(BF16)
            acc = acc + _dot(act, wd_s[cs, :])
        o_ref[...] = acc


def _post(x, mixes, w_out, out_layer, g, wg, wu, wd, layer, *, tm, ff_chunk):
    b, s, d = x.shape
    seq_tiles = s // tm

    def tile(t):
        tt = jnp.maximum(t - W_CAST_STEPS, 0)
        return tt // seq_tiles, lax.rem(tt, seq_tiles)

    def chunk_spec(w, lyr):
        rows = w.shape[1] // W_CAST_STEPS
        return pl.BlockSpec((None, rows, w.shape[2]),
                            lambda t: (lyr, jnp.minimum(t, W_CAST_STEPS - 1), 0))

    tok = pl.BlockSpec((None, tm, d), lambda t: (*tile(t), 0))
    mix_specs = [pl.BlockSpec((None, tm, m.shape[2]), lambda t: (*tile(t), 0)) for m in mixes]
    return pl.pallas_call(
        functools.partial(_post_kernel, n_mix=len(mixes), ff_chunk=ff_chunk),
        grid=(W_CAST_STEPS + b * seq_tiles,),
        in_specs=[tok] + mix_specs
                 + [chunk_spec(w_out, out_layer), _full_spec(g), chunk_spec(wg, layer),
                    chunk_spec(wu, layer), chunk_spec(wd, layer)],
        out_specs=tok,
        out_shape=jax.ShapeDtypeStruct(x.shape, x.dtype),
        scratch_shapes=[pltpu.VMEM(w.shape[1:], BF16) for w in (w_out, wg, wu, wd)],
        compiler_params=_params(1),
        name="outproj_ffn",
    )(x, *mixes, w_out, g, wg, wu, wd)


def _rope_tables(pos_ref, invf_ref):
    half = QK_ROPE // 2
    groups = LANES // half
    rows = pos_ref.shape[0]
    pos = pos_ref[...].astype(F32)
    lane = lax.broadcasted_iota(jnp.int32, (rows, LANES), 1)
    p = pos[:, 0:1]
    for grp in range(1, groups):
        p = jnp.where(lane >= grp * half, pos[:, grp:grp + 1], p)
    ang = p * invf_ref[...]
    tables = []
    for packed in (jnp.cos(ang), jnp.sin(ang)):
        quarters = []
        for grp in range(groups):
            r = packed if grp == 0 else pltpu.roll(packed, LANES - grp * half, 1)
            quarters.append(jnp.where(lane < half, r, pltpu.roll(r, half, 1)))
        tables.append(jnp.concatenate(quarters, axis=0))
    return tables


def _shifted(buf_ref, shift, tm, cs):
    return buf_ref[POOL_CARRY_ROWS - shift:POOL_CARRY_ROWS - shift + tm, cs]


POOL_W = len(POOL_WINDOWS) * POOL_GROUP_DIM
ST_ZP = 0
ST_KR = ST_ZP + POOL_W
ST_Q = ST_KR + LANES
ST_KV = ST_Q + 2 * MLA_HEADS * LANES
ST_WIDTH = ST_KV + MLA_HEADS * (QK_NOPE + V_DIM)


def _odd_pre_matmuls(x_ref, g_ref, w_in_ref, qa_g_ref, qb_ref, kva_g_ref, kvb_ref, st_ref):
    q_lora = qa_g_ref.shape[1]
    kv_lora = kva_g_ref.shape[1]
    h = _rms_norm(x_ref[...], g_ref[...]).astype(BF16)
    proj = _dot(h, w_in_ref[...])
    q_lat = proj[:, POOL_W:POOL_W + q_lora]
    kv_lat = proj[:, POOL_W + q_lora:POOL_W + q_lora + kv_lora]
    kr_off = POOL_W + q_lora + kv_lora
    st_ref[:, ST_ZP:ST_KR] = proj[:, 0:POOL_W]
    st_ref[:, ST_KR:ST_Q] = proj[:, kr_off:kr_off + LANES]
    qn = _rms_norm(q_lat, qa_g_ref[...]).astype(BF16)
    st_ref[:, ST_Q:ST_KV] = _dot(qn, qb_ref[...])
    kvn = _rms_norm(kv_lat, kva_g_ref[...]).astype(BF16)
    st_ref[:, ST_KV:ST_WIDTH] = _dot(kvn, kvb_ref[...])


def _odd_pre_tail(st_ref, pos_ref, invf_ref, pool_w_ref, pool_scale_ref, qg_n_ref, qg_r_ref,
                  qg_sw_ref, kg_n_ref, kg_r_ref, kg_sw_ref, c_ref, q_ref, k_ref, v_ref,
                  bufs, seq_tile, tm):
    pool_w = POOL_W
    zbuf_ref, s2buf_ref, s4buf_ref, s8buf_ref = bufs
    zp = st_ref[:, ST_ZP:ST_KR]
    kr = st_ref[:, ST_KR:ST_Q]

    lo, hi = slice(0, LANES), slice(LANES, 2 * LANES)

    for buf in bufs:
        buf[0:POOL_CARRY_ROWS, :] = jnp.where(seq_tile == 0, 0.0, buf[0:POOL_CARRY_ROWS, :])

    body = slice(POOL_CARRY_ROWS, POOL_CARRY_ROWS + tm)
    zbuf_ref[body, :] = zp
    s2 = zp + _shifted(zbuf_ref, 1, tm, slice(0, pool_w))
    s2buf_ref[body, :] = s2
    s4 = s2 + _shifted(s2buf_ref, 2, tm, slice(0, pool_w))
    s4buf_ref[body, :] = s4[:, hi]
    s8 = s4[:, hi] + _shifted(s4buf_ref, 4, tm, lo)
    s8buf_ref[body, :] = s8
    s16 = s8 + _shifted(s8buf_ref, 8, tm, lo)
    for buf in bufs:
        buf[0:POOL_CARRY_ROWS, :] = buf[tm:tm + POOL_CARRY_ROWS, :]
    lane = lax.broadcasted_iota(jnp.int32, (tm, LANES), 1)
    first = lane < POOL_GROUP_DIM
    t1 = (seq_tile * tm + 1
          + lax.broadcasted_iota(jnp.int32, (tm, 1), 0)).astype(F32)
    inv = [1.0 / jnp.minimum(t1, float(w)) for w in POOL_WINDOWS]
    mean_lo = jnp.where(first, s2[:, lo] * inv[0], s4[:, lo] * inv[1])
    mean_hi = jnp.where(first, s8 * inv[2], s16 * inv[3])
    pooled = (jnp.concatenate([mean_lo, mean_hi], axis=1) - zp).astype(BF16)
    c_ref[...] = (_dot(pooled, pool_w_ref[...]) * pool_scale_ref[...]).astype(BF16)

    cosv, sinv = _rope_tables(pos_ref, invf_ref)

    def rope(pair, gain_cos, gain_sin):
        return pair * gain_cos + pltpu.roll(pair, QK_ROPE, 1) * gain_sin

    scale = QK_DIM ** -0.5 * math.log2(math.e)
    rope_w = MLA_HEADS * LANES
    q_gc, q_gs = qg_r_ref[...] * cosv, qg_sw_ref[...] * sinv
    kr_ss = 0.5 * jnp.sum(kr * kr, axis=-1, keepdims=True)
    kr_rot = rope(kr, kg_r_ref[...] * cosv, kg_sw_ref[...] * sinv)
    for hd in range(MLA_HEADS):
        q0 = ST_Q + hd * LANES
        q_n = st_ref[:, q0:q0 + LANES]
        q_pair = st_ref[:, q0 + rope_w:q0 + rope_w + LANES]
        ss = jnp.sum(q_n * q_n + 0.5 * (q_pair * q_pair), axis=-1, keepdims=True)
        rinv = lax.rsqrt(ss * (1.0 / QK_DIM) + EPS) * scale
        q_ref[hd, :, 0:QK_NOPE] = (q_n * qg_n_ref[...] * rinv).astype(BF16)
        q_ref[hd, :, QK_NOPE:] = (rope(q_pair, q_gc, q_gs) * rinv).astype(BF16)

        k0 = ST_KV + hd * (QK_NOPE + V_DIM)
        k_n = st_ref[:, k0:k0 + QK_NOPE]
        ss = jnp.sum(k_n * k_n, axis=-1, keepdims=True) + kr_ss
        rinv = lax.rsqrt(ss * (1.0 / QK_DIM) + EPS)
        k_ref[hd, :, 0:QK_NOPE] = (k_n * kg_n_ref[...] * rinv).astype(BF16)
        k_ref[hd, :, QK_NOPE:] = (kr_rot * rinv).astype(BF16)
        v_ref[hd] = st_ref[:, k0 + QK_NOPE:k0 + QK_NOPE + V_DIM].astype(BF16)


def _odd_pre_kernel(x_ref, pos_ref, invf_ref, g_ref, w_in_ref, pool_w_ref, pool_scale_ref,
                    qa_g_ref, qb_ref, kva_g_ref, kvb_ref, qg_n_ref, qg_r_ref, qg_sw_ref,
                    kg_n_ref, kg_r_ref, kg_sw_ref,
                    c_ref, q_ref, k_ref, v_ref, st_ref, zbuf_ref, s2buf_ref, s4buf_ref, s8buf_ref,
                    *, tm, seq_tiles):
    t = pl.program_id(0)
    seq_tile = lax.rem(jnp.maximum(t - 1, 0), seq_tiles)
    bufs = (zbuf_ref, s2buf_ref, s4buf_ref, s8buf_ref)

    @pl.when(t == 0)
    def _():
        st_ref[1] = jnp.zeros(st_ref.shape[1:], F32)
        for buf in bufs:
            buf[0:POOL_CARRY_ROWS, :] = jnp.zeros((POOL_CARRY_ROWS, buf.shape[1]), F32)

    def step(slot):
        _odd_pre_matmuls(x_ref, g_ref, w_in_ref, qa_g_ref, qb_ref, kva_g_ref, kvb_ref,
                         st_ref.at[slot])
        _odd_pre_tail(st_ref.at[1 - slot], pos_ref, invf_ref, pool_w_ref, pool_scale_ref,
                      qg_n_ref, qg_r_ref, qg_sw_ref, kg_n_ref, kg_r_ref, kg_sw_ref,
                      c_ref, q_ref, k_ref, v_ref, bufs, seq_tile, tm)

    for slot in range(2):
        pl.when(lax.rem(t, 2) == slot)(functools.partial(step, slot))


def _odd_pre(x, positions, g, w_in, pool_w, pool_scale, qa_g, qb, kva_g, kvb,
             qg_n, qg_r, qg_sw, kg_n, kg_r, kg_sw, *, tm):
    b, s, d = x.shape
    pool_width = pool_w.shape[0]
    qk_pad = QK_NOPE + LANES
    groups = LANES // (QK_ROPE // 2)
    seq_tiles = s // tm
    n_tiles = b * seq_tiles
    pos = positions.reshape(b, seq_tiles, groups, tm // groups).transpose(0, 1, 3, 2)
    inv_freq = ROPE_THETA ** (-jnp.arange(0, QK_ROPE, 2, dtype=F32) / QK_ROPE)
    invf = jnp.tile(inv_freq, groups).reshape(1, LANES)

    lead, trail = _skewed_tile_maps(n_tiles, seq_tiles)

    def head_map(t):
        i, j = trail(t)
        return i, 0, j, 0

    consts = [invf, g, w_in, pool_w, pool_scale, qa_g, qb, kva_g, kvb,
              qg_n, qg_r, qg_sw, kg_n, kg_r, kg_sw]
    head_spec = lambda w: pl.BlockSpec((None, MLA_HEADS, tm, w), head_map)
    carry = lambda w: pltpu.VMEM((tm + POOL_CARRY_ROWS, w), F32)
    return pl.pallas_call(
        functools.partial(_odd_pre_kernel, tm=tm, seq_tiles=seq_tiles),
        grid=(n_tiles + 1,),
        in_specs=[pl.BlockSpec((None, tm, d), lambda t: (*lead(t), 0)),
                  pl.BlockSpec((None, None, tm // groups, groups), lambda t: (*trail(t), 0, 0))]
                 + [_full_spec(c) for c in consts],
        out_specs=[pl.BlockSpec((None, tm, pool_width), lambda t: (*trail(t), 0)),
                   head_spec(qk_pad), head_spec(qk_pad), head_spec(V_DIM)],
        out_shape=[jax.ShapeDtypeStruct((b, s, pool_width), BF16),
                   jax.ShapeDtypeStruct((b, MLA_HEADS, s, qk_pad), BF16),
                   jax.ShapeDtypeStruct((b, MLA_HEADS, s, qk_pad), BF16),
                   jax.ShapeDtypeStruct((b, MLA_HEADS, s, V_DIM), BF16)],
        scratch_shapes=[pltpu.VMEM((2, tm, ST_WIDTH), F32),
                        carry(pool_width), carry(pool_width), carry(LANES), carry(LANES)],
        compiler_params=_params(1),
        name="odd_pre",
    )(x, pos, *consts)


def _attn_kernel(q_ref, k_ref, v_ref, o_ref, *, tq, tk, hp):
    s_len = q_ref.shape[1]
    masks = {}

    def causal_mask(nk, offset):
        if (nk, offset) not in masks:
            row = lax.broadcasted_iota(jnp.int32, (tq, nk), 0)
            col = lax.broadcasted_iota(jnp.int32, (tq, nk), 1)
            masks[(nk, offset)] = row + offset >= col
        return masks[(nk, offset)]

    def tile(hd, q0, k0, nk, state):
        q = q_ref[hd, q0:q0 + tq, :]
        k = k_ref[hd, k0:k0 + nk, :]
        v = v_ref[hd, k0:k0 + nk, :]
        sc = lax.dot_general(q, k, (((1,), (1,)), ((), ())), preferred_element_type=F32)
        if k0 + nk - 1 > q0:
            sc = jnp.where(causal_mask(nk, q0 - k0), sc, MASK_VALUE)
        cols = [sc[:, c * LANES:(c + 1) * LANES] for c in range(nk // LANES)]
        rmax = jnp.max(functools.reduce(jnp.maximum, cols), axis=-1, keepdims=True)
        if state is None:
            m_new = jnp.broadcast_to(rmax, (tq, LANES))
            ps = [jnp.exp2(c - m_new) for c in cols]
            return m_new, functools.reduce(jnp.add, ps), _dot(_bf16_cat(ps), v)
        m_prev, l_prev, acc_prev = state
        m_new = jnp.maximum(m_prev, rmax)
        alpha = jnp.exp2(m_prev - m_new)
        ps = [jnp.exp2(c - m_new) for c in cols]
        return (m_new, alpha * l_prev + functools.reduce(jnp.add, ps),
                alpha * acc_prev + _dot(_bf16_cat(ps), v))

    for qi in range(s_len // tq):
        for hd in range(hp):
            q0 = qi * tq
            state = None
            for k0 in range(0, q0 + tq, tk):
                state = tile(hd, q0, k0, min(tk, q0 + tq - k0), state)
            _, l_part, acc = state
            l = jnp.sum(l_part, axis=-1, keepdims=True)
            o_ref[qi * tq:(qi + 1) * tq, hd * V_DIM:(hd + 1) * V_DIM] = (acc / l).astype(BF16)


def _bf16_cat(cols):
    return jnp.concatenate(cols, axis=1).astype(BF16)


def _attention(q, k, v, *, tq, tk, hp):
    b, nh, s, qk_pad = q.shape
    qk_spec = pl.BlockSpec((None, hp, s, qk_pad), lambda i, j: (i, j, 0, 0))
    return pl.pallas_call(
        functools.partial(_attn_kernel, tq=tq, tk=tk, hp=hp),
        grid=(b, nh // hp),
        in_specs=[qk_spec, qk_spec,
                  pl.BlockSpec((None, hp, s, V_DIM), lambda i, j: (i, j, 0, 0))],
        out_specs=pl.BlockSpec((None, s, hp * V_DIM), lambda i, j: (i, 0, j)),
        out_shape=jax.ShapeDtypeStruct((b, s, nh * V_DIM), BF16),
        compiler_params=_params(),
        name="mla_attention",
    )(q, k, v)


def _pad_lanes(a, width):
    return jnp.pad(a, [(0, 0)] * (a.ndim - 1) + [(0, width - a.shape[-1])])


def _swap_halves(a, *, negate_first):
    a1, a2 = jnp.split(a, 2, axis=-1)
    return jnp.concatenate([-a2 if negate_first else a2, a1], axis=-1)


def _rope_pair(w):
    return jnp.concatenate([w, _swap_halves(w, negate_first=True)], axis=-1)


class _TileConfig(NamedTuple):
    tm_even: int
    tm_odd: int
    tm_ffn: int
    ff_chunk: int
    tq: int
    tk: int
    heads_per_step: int


def _tile_config(seq_len):
    cfg = _TileConfig(tm_even=1024, tm_odd=512, tm_ffn=1024, ff_chunk=1 * MXU_WIDTH,
                      tq=256, tk=256, heads_per_step=2)
    assert all(seq_len % t == 0 for t in (cfg.tm_even, cfg.tm_odd, cfg.tm_ffn, cfg.tq))
    assert cfg.tk % cfg.tq == 0
    assert MLA_HEADS % cfg.heads_per_step == 0 and cfg.tm_even % SG_CHUNK == 0
    return cfg


def kernel(x, positions, mix_norm, ffn_norm, even_w_in, sg_ln_g, sg_w_s, sg_b_s, sc_conv_w,
           even_w_out, odd_w_in, pool_w, pool_scale, q_a_norm, q_b, kv_a_norm, kv_b, q_norm,
           k_norm, odd_w_out, ffn_w_gate, ffn_w_up, ffn_w_down):
    depth = mix_norm.shape[0]
    cfg = _tile_config(x.shape[1])
    row = lambda a: a.reshape(1, -1)

    for layer in range(depth):
        i = layer // 2
        if layer % 2 == 0:
            mix = _even_mixer(
                x, row(mix_norm[layer]), even_w_in[i].astype(BF16), row(sg_ln_g[i]),
                sg_w_s[i], sg_b_s[i].T, sc_conv_w[i], tm=cfg.tm_even)
            mixes, w_out = [mix], even_w_out
        else:
            pool_width = pool_scale.shape[-1]
            n_groups = pool_w.shape[1]
            gd = pool_w.shape[2]
            w_bd = jnp.zeros((pool_width, pool_width), F32)
            for gidx in range(n_groups):
                w_bd = w_bd.at[gidx * gd:(gidx + 1) * gd, gidx * gd:(gidx + 1) * gd].set(pool_w[i, gidx])
            q_lora = q_b.shape[1]
            qb3 = q_b[i].reshape(q_lora, MLA_HEADS, QK_DIM)
            qb_nope = qb3[:, :, :QK_NOPE].reshape(q_lora, MLA_HEADS * QK_NOPE)
            qb_pair = _rope_pair(qb3[:, :, QK_NOPE:]).reshape(q_lora, MLA_HEADS * LANES)
            qb = jnp.concatenate([qb_nope, qb_pair], axis=1).astype(BF16)
            w_in = jnp.concatenate(
                [odd_w_in[i][:, :-QK_ROPE], _rope_pair(odd_w_in[i][:, -QK_ROPE:])],
                axis=1).astype(BF16)
            gain_rows = []
            for gn in (q_norm[i], k_norm[i]):
                g_rope = row(gn[QK_NOPE:])
                gain_rows += [row(gn[:QK_NOPE]), _pad_lanes(g_rope, LANES),
                              _pad_lanes(_swap_halves(g_rope, negate_first=False), LANES)]
            c_out, q, k, v = _odd_pre(
                x, positions, row(mix_norm[layer]), w_in, w_bd.astype(BF16), row(pool_scale[i]),
                row(q_a_norm[i]), qb, row(kv_a_norm[i]), kv_b[i].astype(BF16), *gain_rows,
                tm=cfg.tm_odd)
            d_out = _attention(q, k, v, tq=cfg.tq, tk=cfg.tk, hp=cfg.heads_per_step)
            mixes, w_out = [c_out, d_out], odd_w_out
        x = _post(x, mixes, w_out, i, row(ffn_norm[layer]), ffn_w_gate, ffn_w_up, ffn_w_down,
                  layer, tm=cfg.tm_ffn, ff_chunk=cfg.ff_chunk)
    return x
```

```python
import functools
import math
from typing import NamedTuple

import jax
import jax.numpy as jnp
from jax import lax
from jax.experimental import pallas as pl
from jax.experimental.pallas import tpu as pltpu

F32 = jnp.float32
BF16 = jnp.bfloat16

EPS = 1e-6
MASK_VALUE = -1e30
LANES = 128
MXU_WIDTH = 256
SG_CHUNK = 128
SG_HEADS = 4
CONV_WIDTH = 3
POOL_WINDOWS = (2, 4, 8, 16)
POOL_GROUP_DIM = 64
MLA_HEADS = 6
QK_NOPE = 128
QK_ROPE = 64
QK_DIM = QK_NOPE + QK_ROPE
V_DIM = 128
ROPE_THETA = 10000.0
CARRY_ROWS = 8
POOL_CARRY_ROWS = 16
VMEM_LIMIT_BYTES = 58 * 1024 * 1024


def _rms_norm(x, g):
    ms = jnp.mean(x * x, axis=-1, keepdims=True)
    return x * lax.rsqrt(ms + EPS) * g


def _gelu(x):
    return 0.5 * x * (1.0 + lax.erf(x * math.sqrt(0.5)))


def _dot(a, b):
    return jnp.dot(a, b, preferred_element_type=F32)


def _full_spec(arr):
    nd = arr.ndim
    return pl.BlockSpec(arr.shape, lambda *_: (0,) * nd)


def _params(grid_rank=2):
    return pltpu.CompilerParams(
        dimension_semantics=("arbitrary",) * grid_rank,
        vmem_limit_bytes=VMEM_LIMIT_BYTES)


def _even_mixer_kernel(x_ref, g_ref, w_in_ref, ln_g_ref, w_s_ref, b_st_ref,
                       conv_w_ref, mix_ref, zbuf_ref, *, tm):
    sg_w = SG_HEADS * LANES
    h = _rms_norm(x_ref[...], g_ref[...]).astype(BF16)
    proj = _dot(h, w_in_ref[...])
    u = _gelu(proj[:, 0:sg_w])
    v = _gelu(proj[:, sg_w:2 * sg_w])
    sc_w = (proj.shape[1] - 2 * sg_w) // 3
    b_gate = proj[:, 2 * sg_w:2 * sg_w + sc_w]
    c_gate = proj[:, 2 * sg_w + sc_w:2 * sg_w + 2 * sc_w]
    hv = proj[:, 2 * sg_w + 2 * sc_w:]

    row = lax.broadcasted_iota(jnp.int32, (SG_CHUNK, SG_CHUNK), 0)
    col = lax.broadcasted_iota(jnp.int32, (SG_CHUNK, SG_CHUNK), 1)
    causal = row >= col
    for hd in range(SG_HEADS):
        cs = slice(hd * LANES, (hd + 1) * LANES)
        vh = v[:, cs]
        mu = jnp.mean(vh, axis=-1, keepdims=True)
        xc = vh - mu
        var = jnp.mean(xc * xc, axis=-1, keepdims=True)
        vn = (xc * lax.rsqrt(var + EPS) * ln_g_ref[:, cs]).astype(BF16)
        w = jnp.where(causal, w_s_ref[hd], 0.0).astype(BF16)
        bias = b_st_ref[:, hd:hd + 1]
        for c in range(tm // SG_CHUNK):
            rs = slice(c * SG_CHUNK, (c + 1) * SG_CHUNK)
            mixed = _dot(w, vn[rs]) + bias
            mix_ref[rs, cs] = (u[rs, cs] * mixed).astype(BF16)

    @pl.when(pl.program_id(1) == 0)
    def _():
        zbuf_ref[0:CARRY_ROWS, :] = jnp.zeros((CARRY_ROWS, sc_w), F32)

    z = c_gate * hv
    zbuf_ref[CARRY_ROWS:CARRY_ROWS + tm, :] = z
    y = conv_w_ref[CONV_WIDTH - 1:CONV_WIDTH, :] * z
    for k in range(CONV_WIDTH - 1):
        shift = CONV_WIDTH - 1 - k
        y = y + conv_w_ref[k:k + 1, :] * zbuf_ref[CARRY_ROWS - shift:CARRY_ROWS - shift + tm, :]
    mix_ref[:, sg_w:] = (b_gate * y).astype(BF16)
    zbuf_ref[0:CARRY_ROWS, :] = zbuf_ref[tm:tm + CARRY_ROWS, :]


def _skewed_tile_maps(n_tiles, seq_tiles):
    def lead(t):
        tt = jnp.minimum(t, n_tiles - 1)
        return tt // seq_tiles, lax.rem(tt, seq_tiles)

    def trail(t):
        tt = jnp.maximum(t - 1, 0)
        return tt // seq_tiles, lax.rem(tt, seq_tiles)

    return lead, trail


def _even_mixer(x, g, w_in, ln_g, w_s, b_st, conv_w, *, tm):
    b, s, d = x.shape
    sc_w = conv_w.shape[1]
    mix_w = SG_HEADS * LANES + sc_w
    tok = pl.BlockSpec((None, tm, d), lambda i, j: (i, j, 0))
    return pl.pallas_call(
        functools.partial(_even_mixer_kernel, tm=tm),
        grid=(b, s // tm),
        in_specs=[tok, _full_spec(g), _full_spec(w_in), _full_spec(ln_g),
                  _full_spec(w_s), _full_spec(b_st), _full_spec(conv_w)],
        out_specs=pl.BlockSpec((None, tm, mix_w), lambda i, j: (i, j, 0)),
        out_shape=jax.ShapeDtypeStruct((b, s, mix_w), BF16),
        scratch_shapes=[pltpu.VMEM((tm + CARRY_ROWS, sc_w), F32)],
        compiler_params=_params(),
        name="even_mixer",
    )(x, g, w_in, ln_g, w_s, b_st, conv_w)


W_CAST_STEPS = 16


def _post_kernel(*refs, n_mix, ff_chunk):
    x_ref = refs[0]
    mix_refs = refs[1:1 + n_mix]
    (wo_ref, g_ref, wg_ref, wu_ref, wd_ref, o_ref,
     wo_s, wg_s, wu_s, wd_s) = refs[1 + n_mix:]
    step = pl.program_id(0)

    @pl.when(step < W_CAST_STEPS)
    def _():
        for src, dst in ((wo_ref, wo_s), (wg_ref, wg_s), (wu_ref, wu_s), (wd_ref, wd_s)):
            rows = src.shape[0]
            dst[pl.ds(pl.multiple_of(step * rows, rows), rows), :] = src[...].astype(BF16)

    @pl.when(step >= W_CAST_STEPS)
    def _():
        mix = [m_ref[...] for m_ref in mix_refs]
        mix = mix[0] if n_mix == 1 else jnp.concatenate(mix, axis=1)
        x1 = x_ref[...] + _dot(mix, wo_s[...])
        h = _rms_norm(x1, g_ref[...]).astype(BF16)
        acc = x1
        d_ff = wg_s.shape[1]
        for c0 in range(0, d_ff, ff_chunk):
            cs = slice(c0, min(c0 + ff_chunk, d_ff))
            gate = _dot(h, wg_s[:, cs])
            up = _dot(h, wu_s[:, cs])
            act = (gate / (1.0 + jnp.exp(-gate)) * up).astype(BF16)
            acc = acc + _dot(act, wd_s[cs, :])
        o_ref[...] = acc


def _post(x, mixes, w_out, out_layer, g, wg, wu, wd, layer, *, tm, ff_chunk):
    b, s, d = x.shape
    seq_tiles = s // tm

    def tile(t):
        tt = jnp.maximum(t - W_CAST_STEPS, 0)
        return tt // seq_tiles, lax.rem(tt, seq_tiles)

    def chunk_spec(w, lyr):
        rows = w.shape[1] // W_CAST_STEPS
        return pl.BlockSpec((None, rows, w.shape[2]),
                            lambda t: (lyr, jnp.minimum(t, W_CAST_STEPS - 1), 0))

    tok = pl.BlockSpec((None, tm, d), lambda t: (*tile(t), 0))
    mix_specs = [pl.BlockSpec((None, tm, m.shape[2]), lambda t: (*tile(t), 0)) for m in mixes]
    return pl.pallas_call(
        functools.partial(_post_kernel, n_mix=len(mixes), ff_chunk=ff_chunk),
        grid=(W_CAST_STEPS + b * seq_tiles,),
        in_specs=[tok] + mix_specs
                 + [chunk_spec(w_out, out_layer), _full_spec(g), chunk_spec(wg, layer),
                    chunk_spec(wu, layer), chunk_spec(wd, layer)],
        out_specs=tok,
        out_shape=jax.ShapeDtypeStruct(x.shape, x.dtype),
        scratch_shapes=[pltpu.VMEM(w.shape[1:], BF16) for w in (w_out, wg, wu, wd)],
        compiler_params=_params(1),
        name="outproj_ffn",
    )(x, *mixes, w_out, g, wg, wu, wd)


def _rope_tables(pos_ref, invf_ref):
    half = QK_ROPE // 2
    groups = LANES // half
    rows = pos_ref.shape[0]
    pos = pos_ref[...].astype(F32)
    lane = lax.broadcasted_iota(jnp.int32, (rows, LANES), 1)
    p = pos[:, 0:1]
    for grp in range(1, groups):
        p = jnp.where(lane >= grp * half, pos[:, grp:grp + 1], p)
    ang = p * invf_ref[...]
    tables = []
    for packed in (jnp.cos(ang), jnp.sin(ang)):
        quarters = []
        for grp in range(groups):
            r = packed if grp == 0 else pltpu.roll(packed, LANES - grp * half, 1)
            quarters.append(jnp.where(lane < half, r, pltpu.roll(r, half, 1)))
        tables.append(jnp.concatenate(quarters, axis=0))
    return tables


def _shifted(buf_ref, shift, tm, cs):
    return buf_ref[POOL_CARRY_ROWS - shift:POOL_CARRY_ROWS - shift + tm, cs]


POOL_W = len(POOL_WINDOWS) * POOL_GROUP_DIM
ST_ZP = 0
ST_KR = ST_ZP + POOL_W
ST_Q = ST_KR + LANES
ST_KV = ST_Q + 2 * MLA_HEADS * LANES
ST_WIDTH = ST_KV + MLA_HEADS * (QK_NOPE + V_DIM)


def _odd_pre_matmuls(x_ref, g_ref, w_in_ref, qa_g_ref, qb_ref, kva_g_ref, kvb_ref, st_ref):
    q_lora = qa_g_ref.shape[1]
    kv_lora = kva_g_ref.shape[1]
    h = _rms_norm(x_ref[...], g_ref[...]).astype(BF16)
    proj = _dot(h, w_in_ref[...])
    q_lat = proj[:, POOL_W:POOL_W + q_lora]
    kv_lat = proj[:, POOL_W + q_lora:POOL_W + q_lora + kv_lora]
    kr_off = POOL_W + q_lora + kv_lora
    st_ref[:, ST_ZP:ST_KR] = proj[:, 0:POOL_W]
    st_ref[:, ST_KR:ST_Q] = proj[:, kr_off:kr_off + LANES]
    qn = _rms_norm(q_lat, qa_g_ref[...]).astype(BF16)
    st_ref[:, ST_Q:ST_KV] = _dot(qn, qb_ref[...])
    kvn = _rms_norm(kv_lat, kva_g_ref[...]).astype(BF16)
    st_ref[:, ST_KV:ST_WIDTH] = _dot(kvn, kvb_ref[...])


def _odd_pre_tail(st_ref, pos_ref, invf_ref, pool_w_ref, pool_scale_ref, qg_n_ref, qg_r_ref,
                  qg_sw_ref, kg_n_ref, kg_r_ref, kg_sw_ref, c_ref, q_ref, k_ref, v_ref,
                  bufs, seq_tile, tm):
    pool_w = POOL_W
    zbuf_ref, s2buf_ref, s4buf_ref, s8buf_ref = bufs
    zp = st_ref[:, ST_ZP:ST_KR]
    kr = st_ref[:, ST_KR:ST_Q]

    lo, hi = slice(0, LANES), slice(LANES, 2 * LANES)

    for buf in bufs:
        buf[0:POOL_CARRY_ROWS, :] = jnp.where(seq_tile == 0, 0.0, buf[0:POOL_CARRY_ROWS, :])

    body = slice(POOL_CARRY_ROWS, POOL_CARRY_ROWS + tm)
    zbuf_ref[body, :] = zp
    s2 = zp + _shifted(zbuf_ref, 1, tm, slice(0, pool_w))
    s2buf_ref[body, :] = s2
    s4 = s2 + _shifted(s2buf_ref, 2, tm, slice(0, pool_w))
    s4buf_ref[body, :] = s4[:, hi]
    s8 = s4[:, hi] + _shifted(s4buf_ref, 4, tm, lo)
    s8buf_ref[body, :] = s8
    s16 = s8 + _shifted(s8buf_ref, 8, tm, lo)
    for buf in bufs:
        buf[0:POOL_CARRY_ROWS, :] = buf[tm:tm + POOL_CARRY_ROWS, :]
    lane = lax.broadcasted_iota(jnp.int32, (tm, LANES), 1)
    first = lane < POOL_GROUP_DIM
    sums = jnp.concatenate([jnp.where(first, s2[:, lo], s4[:, lo]),
                            jnp.where(first, s8, s16)], axis=1)
    head = max(POOL_WINDOWS)
    grp = lax.broadcasted_iota(jnp.int32, (head, pool_w), 1) // POOL_GROUP_DIM
    w_row = jnp.full((head, pool_w), float(POOL_WINDOWS[0]), F32)
    for g, w in enumerate(POOL_WINDOWS[1:], 1):
        w_row = jnp.where(grp >= g, float(w), w_row)
    t1 = (lax.broadcasted_iota(jnp.int32, (head, pool_w), 0) + 1).astype(F32)
    mean = sums * (1.0 / w_row[0:1])
    mean_head = jnp.where(seq_tile == 0, sums[0:head] / jnp.minimum(t1, w_row), mean[0:head])
    mean = jnp.concatenate([mean_head, mean[head:]], axis=0)
    pooled = (mean - zp).astype(BF16)
    c_ref[...] = (_dot(pooled, pool_w_ref[...]) * pool_scale_ref[...]).astype(BF16)

    cosv, sinv = _rope_tables(pos_ref, invf_ref)

    def rope(pair, gain_cos, gain_sin):
        return pair * gain_cos + pltpu.roll(pair, QK_ROPE, 1) * gain_sin

    k_const = math.sqrt(QK_DIM)
    q_const = k_const * QK_DIM ** -0.5 * math.log2(math.e)
    ss_eps = QK_DIM * EPS
    rope_w = MLA_HEADS * LANES
    qg_n = qg_n_ref[...] * q_const
    kg_n = kg_n_ref[...] * k_const
    q_gc, q_gs = (qg_r_ref[...] * q_const) * cosv, (qg_sw_ref[...] * q_const) * sinv
    kr_ss = 0.5 * jnp.sum(kr * kr, axis=-1, keepdims=True) + ss_eps
    kr_rot = rope(kr, (kg_r_ref[...] * k_const) * cosv, (kg_sw_ref[...] * k_const) * sinv)
    for hd in range(MLA_HEADS):
        q0 = ST_Q + hd * LANES
        q_n = st_ref[:, q0:q0 + LANES]
        q_pair = st_ref[:, q0 + rope_w:q0 + rope_w + LANES]
        ss = jnp.sum(q_n * q_n + 0.5 * (q_pair * q_pair), axis=-1, keepdims=True)
        rinv = lax.rsqrt(ss + ss_eps)
        q_ref[hd, :, 0:QK_NOPE] = (q_n * qg_n * rinv).astype(BF16)
        q_ref[hd, :, QK_NOPE:] = (rope(q_pair, q_gc, q_gs) * rinv).astype(BF16)

        k0 = ST_KV + hd * (QK_NOPE + V_DIM)
        k_n = st_ref[:, k0:k0 + QK_NOPE]
        rinv = lax.rsqrt(jnp.sum(k_n * k_n, axis=-1, keepdims=True) + kr_ss)
        k_ref[hd, :, 0:QK_NOPE] = (k_n * kg_n * rinv).astype(BF16)
        k_ref[hd, :, QK_NOPE:] = (kr_rot * rinv).astype(BF16)
        v_ref[hd] = st_ref[:, k0 + QK_NOPE:k0 + QK_NOPE + V_DIM].astype(BF16)


def _odd_pre_kernel(x_ref, pos_ref, invf_ref, g_ref, w_in_ref, pool_w_ref, pool_scale_ref,
                    qa_g_ref, qb_ref, kva_g_ref, kvb_ref, qg_n_ref, qg_r_ref, qg_sw_ref,
                    kg_n_ref, kg_r_ref, kg_sw_ref,
                    c_ref, q_ref, k_ref, v_ref, st_ref, zbuf_ref, s2buf_ref, s4buf_ref, s8buf_ref,
                    *, tm, seq_tiles):
    t = pl.program_id(0)
    seq_tile = lax.rem(jnp.maximum(t - 1, 0), seq_tiles)
    bufs = (zbuf_ref, s2buf_ref, s4buf_ref, s8buf_ref)

    @pl.when(t == 0)
    def _():
        st_ref[1] = jnp.zeros(st_ref.shape[1:], F32)
        for buf in bufs:
            buf[0:POOL_CARRY_ROWS, :] = jnp.zeros((POOL_CARRY_ROWS, buf.shape[1]), F32)

    def step(slot):
        _odd_pre_matmuls(x_ref, g_ref, w_in_ref, qa_g_ref, qb_ref, kva_g_ref, kvb_ref,
                         st_ref.at[slot])
        _odd_pre_tail(st_ref.at[1 - slot], pos_ref, invf_ref, pool_w_ref, pool_scale_ref,
                      qg_n_ref, qg_r_ref, qg_sw_ref, kg_n_ref, kg_r_ref, kg_sw_ref,
                      c_ref, q_ref, k_ref, v_ref, bufs, seq_tile, tm)

    for slot in range(2):
        pl.when(lax.rem(t, 2) == slot)(functools.partial(step, slot))


def _odd_pre(x, positions, g, w_in, pool_w, pool_scale, qa_g, qb, kva_g, kvb,
             qg_n, qg_r, qg_sw, kg_n, kg_r, kg_sw, *, tm):
    b, s, d = x.shape
    pool_width = pool_w.shape[0]
    qk_pad = QK_NOPE + LANES
    groups = LANES // (QK_ROPE // 2)
    seq_tiles = s // tm
    n_tiles = b * seq_tiles
    pos = positions.reshape(b, seq_tiles, groups, tm // groups).transpose(0, 1, 3, 2)
    inv_freq = ROPE_THETA ** (-jnp.arange(0, QK_ROPE, 2, dtype=F32) / QK_ROPE)
    invf = jnp.tile(inv_freq, groups).reshape(1, LANES)

    lead, trail = _skewed_tile_maps(n_tiles, seq_tiles)

    def head_map(t):
        i, j = trail(t)
        return i, 0, j, 0

    consts = [invf, g, w_in, pool_w, pool_scale, qa_g, qb, kva_g, kvb,
              qg_n, qg_r, qg_sw, kg_n, kg_r, kg_sw]
    head_spec = lambda w: pl.BlockSpec((None, MLA_HEADS, tm, w), head_map)
    carry = lambda w: pltpu.VMEM((tm + POOL_CARRY_ROWS, w), F32)
    return pl.pallas_call(
        functools.partial(_odd_pre_kernel, tm=tm, seq_tiles=seq_tiles),
        grid=(n_tiles + 1,),
        in_specs=[pl.BlockSpec((None, tm, d), lambda t: (*lead(t), 0)),
                  pl.BlockSpec((None, None, tm // groups, groups), lambda t: (*trail(t), 0, 0))]
                 + [_full_spec(c) for c in consts],
        out_specs=[pl.BlockSpec((None, tm, pool_width), lambda t: (*trail(t), 0)),
                   head_spec(qk_pad), head_spec(qk_pad), head_spec(V_DIM)],
        out_shape=[jax.ShapeDtypeStruct((b, s, pool_width), BF16),
                   jax.ShapeDtypeStruct((b, MLA_HEADS, s, qk_pad), BF16),
                   jax.ShapeDtypeStruct((b, MLA_HEADS, s, qk_pad), BF16),
                   jax.ShapeDtypeStruct((b, MLA_HEADS, s, V_DIM), BF16)],
        scratch_shapes=[pltpu.VMEM((2, tm, ST_WIDTH), F32),
                        carry(pool_width), carry(pool_width), carry(LANES), carry(LANES)],
        compiler_params=_params(1),
        name="odd_pre",
    )(x, pos, *consts)


def _attn_kernel(q_ref, k_ref, v_ref, o_ref, *, tq, tk, hp):
    s_len = q_ref.shape[1]
    masks = {}

    def causal_mask(nk, offset):
        if (nk, offset) not in masks:
            row = lax.broadcasted_iota(jnp.int32, (tq, nk), 0)
            col = lax.broadcasted_iota(jnp.int32, (tq, nk), 1)
            masks[(nk, offset)] = row + offset >= col
        return masks[(nk, offset)]

    def tile(hd, q0, k0, nk, state):
        q = q_ref[hd, q0:q0 + tq, :]
        k = k_ref[hd, k0:k0 + nk, :]
        v = v_ref[hd, k0:k0 + nk, :]
        sc = lax.dot_general(q, k, (((1,), (1,)), ((), ())), preferred_element_type=F32)
        if k0 + nk - 1 > q0:
            sc = jnp.where(causal_mask(nk, q0 - k0), sc, MASK_VALUE)
        cols = [sc[:, c * LANES:(c + 1) * LANES] for c in range(nk // LANES)]
        rmax = jnp.max(functools.reduce(jnp.maximum, cols), axis=-1, keepdims=True)
        if state is None:
            m_new = jnp.broadcast_to(rmax, (tq, LANES))
            ps = [jnp.exp2(c - m_new) for c in cols]
            return m_new, functools.reduce(jnp.add, ps), _dot(_bf16_cat(ps), v)
        m_prev, l_prev, acc_prev = state
        m_new = jnp.maximum(m_prev, rmax)
        alpha = jnp.exp2(m_prev - m_new)
        ps = [jnp.exp2(c - m_new) for c in cols]
        return (m_new, alpha * l_prev + functools.reduce(jnp.add, ps),
                alpha * acc_prev + _dot(_bf16_cat(ps), v))

    for qi in range(s_len // tq):
        for hd in range(hp):
            q0 = qi * tq
            state = None
            for k0 in range(0, q0 + tq, tk):
                state = tile(hd, q0, k0, min(tk, q0 + tq - k0), state)
            _, l_part, acc = state
            l = jnp.sum(l_part, axis=-1, keepdims=True)
            o_ref[qi * tq:(qi + 1) * tq, hd * V_DIM:(hd + 1) * V_DIM] = (acc / l).astype(BF16)


def _bf16_cat(cols):
    return jnp.concatenate(cols, axis=1).astype(BF16)


def _attention(q, k, v, *, tq, tk, hp):
    b, nh, s, qk_pad = q.shape
    qk_spec = pl.BlockSpec((None, hp, s, qk_pad), lambda i, j: (i, j, 0, 0))
    return pl.pallas_call(
        functools.partial(_attn_kernel, tq=tq, tk=tk, hp=hp),
        grid=(b, nh // hp),
        in_specs=[qk_spec, qk_spec,
                  pl.BlockSpec((None, hp, s, V_DIM), lambda i, j: (i, j, 0, 0))],
        out_specs=pl.BlockSpec((None, s, hp * V_DIM), lambda i, j: (i, 0, j)),
        out_shape=jax.ShapeDtypeStruct((b, s, nh * V_DIM), BF16),
        compiler_params=_params(),
        name="mla_attention",
    )(q, k, v)


def _pad_lanes(a, width):
    return jnp.pad(a, [(0, 0)] * (a.ndim - 1) + [(0, width - a.shape[-1])])


def _swap_halves(a, *, negate_first):
    a1, a2 = jnp.split(a, 2, axis=-1)
    return jnp.concatenate([-a2 if negate_first else a2, a1], axis=-1)


def _rope_pair(w):
    return jnp.concatenate([w, _swap_halves(w, negate_first=True)], axis=-1)


class _TileConfig(NamedTuple):
    tm_even: int
    tm_odd: int
    tm_ffn: int
    ff_chunk: int
    tq: int
    tk: int
    heads_per_step: int


def _tile_config(seq_len):
    cfg = _TileConfig(tm_even=1024, tm_odd=512, tm_ffn=1024, ff_chunk=1 * MXU_WIDTH,
                      tq=256, tk=256, heads_per_step=2)
    assert all(seq_len % t == 0 for t in (cfg.tm_even, cfg.tm_odd, cfg.tm_ffn, cfg.tq))
    assert cfg.tk % cfg.tq == 0
    assert MLA_HEADS % cfg.heads_per_step == 0 and cfg.tm_even % SG_CHUNK == 0
    return cfg


def kernel(x, positions, mix_norm, ffn_norm, even_w_in, sg_ln_g, sg_w_s, sg_b_s, sc_conv_w,
           even_w_out, odd_w_in, pool_w, pool_scale, q_a_norm, q_b, kv_a_norm, kv_b, q_norm,
           k_norm, odd_w_out, ffn_w_gate, ffn_w_up, ffn_w_down):
    depth = mix_norm.shape[0]
    cfg = _tile_config(x.shape[1])
    row = lambda a: a.reshape(1, -1)

    for layer in range(depth):
        i = layer // 2
        if layer % 2 == 0:
            mix = _even_mixer(
                x, row(mix_norm[layer]), even_w_in[i].astype(BF16), row(sg_ln_g[i]),
                sg_w_s[i], sg_b_s[i].T, sc_conv_w[i], tm=cfg.tm_even)
            mixes, w_out = [mix], even_w_out
        else:
            pool_width = pool_scale.shape[-1]
            n_groups = pool_w.shape[1]
            gd = pool_w.shape[2]
            w_bd = jnp.zeros((pool_width, pool_width), F32)
            for gidx in range(n_groups):
                w_bd = w_bd.at[gidx * gd:(gidx + 1) * gd, gidx * gd:(gidx + 1) * gd].set(pool_w[i, gidx])
            q_lora = q_b.shape[1]
            qb3 = q_b[i].reshape(q_lora, MLA_HEADS, QK_DIM)
            qb_nope = qb3[:, :, :QK_NOPE].reshape(q_lora, MLA_HEADS * QK_NOPE)
            qb_pair = _rope_pair(qb3[:, :, QK_NOPE:]).reshape(q_lora, MLA_HEADS * LANES)
            qb = jnp.concatenate([qb_nope, qb_pair], axis=1).astype(BF16)
            w_in = jnp.concatenate(
                [odd_w_in[i][:, :-QK_ROPE], _rope_pair(odd_w_in[i][:, -QK_ROPE:])],
                axis=1).astype(BF16)
            gain_rows = []
            for gn in (q_norm[i], k_norm[i]):
                g_rope = row(gn[QK_NOPE:])
                gain_rows += [row(gn[:QK_NOPE]), _pad_lanes(g_rope, LANES),
                              _pad_lanes(_swap_halves(g_rope, negate_first=False), LANES)]
            c_out, q, k, v = _odd_pre(
                x, positions, row(mix_norm[layer]), w_in, w_bd.astype(BF16), row(pool_scale[i]),
                row(q_a_norm[i]), qb, row(kv_a_norm[i]), kv_b[i].astype(BF16), *gain_rows,
                tm=cfg.tm_odd)
            d_out = _attention(q, k, v, tq=cfg.tq, tk=cfg.tk, hp=cfg.heads_per_step)
            mixes, w_out = [c_out, d_out], odd_w_out
        x = _post(x, mixes, w_out, i, row(ffn_norm[layer]), ffn_w_gate, ffn_w_up, ffn_w_down,
                  layer, tm=cfg.tm_ffn, ff_chunk=cfg.ff_chunk)
    return x
```

```python
import functools
import math
from typing import NamedTuple

import jax
import jax.numpy as jnp
from jax import lax
from jax.experimental import pallas as pl
from jax.experimental.pallas import tpu as pltpu

F32 = jnp.float32
BF16 = jnp.bfloat16

EPS = 1e-6
MASK_VALUE = -1e30
LANES = 128
MXU_WIDTH = 256
SG_CHUNK = 128
SG_HEADS = 4
CONV_WIDTH = 3
POOL_WINDOWS = (2, 4, 8, 16)
POOL_GROUP_DIM = 64
MLA_HEADS = 6
QK_NOPE = 128
QK_ROPE = 64
QK_DIM = QK_NOPE + QK_ROPE
V_DIM = 128
ROPE_THETA = 10000.0
CARRY_ROWS = 8
POOL_CARRY_ROWS = 16
VMEM_LIMIT_BYTES = 58 * 1024 * 1024


def _rms_norm(x, g):
    ms = jnp.mean(x * x, axis=-1, keepdims=True)
    return x * lax.rsqrt(ms + EPS) * g


def _gelu(x):
    return 0.5 * x * (1.0 + lax.erf(x * math.sqrt(0.5)))


def _dot(a, b):
    return jnp.dot(a, b, preferred_element_type=F32)


def _full_spec(arr):
    nd = arr.ndim
    return pl.BlockSpec(arr.shape, lambda *_: (0,) * nd)


def _params(grid_rank=2):
    return pltpu.CompilerParams(
        dimension_semantics=("arbitrary",) * grid_rank,
        vmem_limit_bytes=VMEM_LIMIT_BYTES)


def _even_mixer_kernel(x_ref, g_ref, w_in_ref, ln_g_ref, w_s_ref, b_st_ref,
                       conv_w_ref, mix_ref, zbuf_ref, *, tm):
    sg_w = SG_HEADS * LANES
    h = _rms_norm(x_ref[...], g_ref[...]).astype(BF16)
    proj = _dot(h, w_in_ref[...])
    u = _gelu(proj[:, 0:sg_w])
    v = _gelu(proj[:, sg_w:2 * sg_w])
    sc_w = (proj.shape[1] - 2 * sg_w) // 3
    b_gate = proj[:, 2 * sg_w:2 * sg_w + sc_w]
    c_gate = proj[:, 2 * sg_w + sc_w:2 * sg_w + 2 * sc_w]
    hv = proj[:, 2 * sg_w + 2 * sc_w:]

    row = lax.broadcasted_iota(jnp.int32, (SG_CHUNK, SG_CHUNK), 0)
    col = lax.broadcasted_iota(jnp.int32, (SG_CHUNK, SG_CHUNK), 1)
    causal = row >= col
    for hd in range(SG_HEADS):
        cs = slice(hd * LANES, (hd + 1) * LANES)
        vh = v[:, cs]
        mu = jnp.mean(vh, axis=-1, keepdims=True)
        xc = vh - mu
        var = jnp.mean(xc * xc, axis=-1, keepdims=True)
        vn = (xc * lax.rsqrt(var + EPS) * ln_g_ref[:, cs]).astype(BF16)
        w = jnp.where(causal, w_s_ref[hd], 0.0).astype(BF16)
        bias = b_st_ref[:, hd:hd + 1]
        for c in range(tm // SG_CHUNK):
            rs = slice(c * SG_CHUNK, (c + 1) * SG_CHUNK)
            mixed = _dot(w, vn[rs]) + bias
            mix_ref[rs, cs] = (u[rs, cs] * mixed).astype(BF16)

    @pl.when(pl.program_id(1) == 0)
    def _():
        zbuf_ref[0:CARRY_ROWS, :] = jnp.zeros((CARRY_ROWS, sc_w), F32)

    z = c_gate * hv
    zbuf_ref[CARRY_ROWS:CARRY_ROWS + tm, :] = z
    y = conv_w_ref[CONV_WIDTH - 1:CONV_WIDTH, :] * z
    for k in range(CONV_WIDTH - 1):
        shift = CONV_WIDTH - 1 - k
        y = y + conv_w_ref[k:k + 1, :] * zbuf_ref[CARRY_ROWS - shift:CARRY_ROWS - shift + tm, :]
    mix_ref[:, sg_w:] = (b_gate * y).astype(BF16)
    zbuf_ref[0:CARRY_ROWS, :] = zbuf_ref[tm:tm + CARRY_ROWS, :]


def _skewed_tile_maps(n_tiles, seq_tiles):
    def lead(t):
        tt = jnp.minimum(t, n_tiles - 1)
        return tt // seq_tiles, lax.rem(tt, seq_tiles)

    def trail(t):
        tt = jnp.maximum(t - 1, 0)
        return tt // seq_tiles, lax.rem(tt, seq_tiles)

    return lead, trail


def _even_mixer(x, g, w_in, ln_g, w_s, b_st, conv_w, *, tm):
    b, s, d = x.shape
    sc_w = conv_w.shape[1]
    mix_w = SG_HEADS * LANES + sc_w
    tok = pl.BlockSpec((None, tm, d), lambda i, j: (i, j, 0))
    return pl.pallas_call(
        functools.partial(_even_mixer_kernel, tm=tm),
        grid=(b, s // tm),
        in_specs=[tok, _full_spec(g), _full_spec(w_in), _full_spec(ln_g),
                  _full_spec(w_s), _full_spec(b_st), _full_spec(conv_w)],
        out_specs=pl.BlockSpec((None, tm, mix_w), lambda i, j: (i, j, 0)),
        out_shape=jax.ShapeDtypeStruct((b, s, mix_w), BF16),
        scratch_shapes=[pltpu.VMEM((tm + CARRY_ROWS, sc_w), F32)],
        compiler_params=_params(),
        name="even_mixer",
    )(x, g, w_in, ln_g, w_s, b_st, conv_w)


W_CAST_STEPS = 16


def _post_kernel(*refs, n_mix, ff_chunk):
    x_ref = refs[0]
    mix_refs = refs[1:1 + n_mix]
    (wo_ref, g_ref, wg_ref, wu_ref, wd_ref, o_ref,
     wo_s, wg_s, wu_s, wd_s) = refs[1 + n_mix:]
    step = pl.program_id(0)

    @pl.when(step < W_CAST_STEPS)
    def _():
        for src, dst in ((wo_ref, wo_s), (wg_ref, wg_s), (wu_ref, wu_s), (wd_ref, wd_s)):
            rows = src.shape[0]
            dst[pl.ds(pl.multiple_of(step * rows, rows), rows), :] = src[...].astype(BF16)

    @pl.when(step >= W_CAST_STEPS)
    def _():
        mix = [m_ref[...] for m_ref in mix_refs]
        mix = mix[0] if n_mix == 1 else jnp.concatenate(mix, axis=1)
        x1 = x_ref[...] + _dot(mix, wo_s[...])
        h = _rms_norm(x1, g_ref[...]).astype(BF16)
        acc = x1
        d_ff = wg_s.shape[1]
        for c0 in range(0, d_ff, ff_chunk):
            cs = slice(c0, min(c0 + ff_chunk, d_ff))
            gate = _dot(h, wg_s[:, cs])
            up = _dot(h, wu_s[:, cs])
            act = (gate / (1.0 + jnp.exp(-gate)) * up).astype(BF16)
            acc = acc + _dot(act, wd_s[cs, :])
        o_ref[...] = acc


def _post(x, mixes, w_out, out_layer, g, wg, wu, wd, layer, *, tm, ff_chunk):
    b, s, d = x.shape
    seq_tiles = s // tm

    def tile(t):
        tt = jnp.maximum(t - W_CAST_STEPS, 0)
        return tt // seq_tiles, lax.rem(tt, seq_tiles)

    def chunk_spec(w, lyr):
        rows = w.shape[1] // W_CAST_STEPS
        return pl.BlockSpec((None, rows, w.shape[2]),
                            lambda t: (lyr, jnp.minimum(t, W_CAST_STEPS - 1), 0))

    tok = pl.BlockSpec((None, tm, d), lambda t: (*tile(t), 0))
    mix_specs = [pl.BlockSpec((None, tm, m.shape[2]), lambda t: (*tile(t), 0)) for m in mixes]
    return pl.pallas_call(
        functools.partial(_post_kernel, n_mix=len(mixes), ff_chunk=ff_chunk),
        grid=(W_CAST_STEPS + b * seq_tiles,),
        in_specs=[tok] + mix_specs
                 + [chunk_spec(w_out, out_layer), _full_spec(g), chunk_spec(wg, layer),
                    chunk_spec(wu, layer), chunk_spec(wd, layer)],
        out_specs=tok,
        out_shape=jax.ShapeDtypeStruct(x.shape, x.dtype),
        scratch_shapes=[pltpu.VMEM(w.shape[1:], BF16) for w in (w_out, wg, wu, wd)],
        compiler_params=_params(1),
        name="outproj_ffn",
    )(x, *mixes, w_out, g, wg, wu, wd)


def _rope_tables(pos_ref, invf_ref):
    half = QK_ROPE // 2
    groups = LANES // half
    rows = pos_ref.shape[0]
    pos = pos_ref[...].astype(F32)
    lane = lax.broadcasted_iota(jnp.int32, (rows, LANES), 1)
    p = pos[:, 0:1]
    for grp in range(1, groups):
        p = jnp.where(lane >= grp * half, pos[:, grp:grp + 1], p)
    ang = p * invf_ref[...]
    tables = []
    for packed in (jnp.cos(ang), jnp.sin(ang)):
        quarters = []
        for grp in range(groups):
            r = packed if grp == 0 else pltpu.roll(packed, LANES - grp * half, 1)
            quarters.append(jnp.where(lane < half, r, pltpu.roll(r, half, 1)))
        tables.append(jnp.concatenate(quarters, axis=0))
    return tables


def _shifted(buf_ref, shift, tm, cs):
    return buf_ref[POOL_CARRY_ROWS - shift:POOL_CARRY_ROWS - shift + tm, cs]


POOL_W = len(POOL_WINDOWS) * POOL_GROUP_DIM
ST_ZP = 0
ST_KR = ST_ZP + POOL_W
ST_Q = ST_KR + LANES
ST_KV = ST_Q + 2 * MLA_HEADS * LANES
ST_WIDTH = ST_KV + MLA_HEADS * (QK_NOPE + V_DIM)


def _odd_pre_matmuls(x_ref, g_ref, w_in_ref, qa_g_ref, qb_ref, kva_g_ref, kvb_ref, st_ref):
    q_lora = qa_g_ref.shape[1]
    kv_lora = kva_g_ref.shape[1]
    h = _rms_norm(x_ref[...], g_ref[...]).astype(BF16)
    proj = _dot(h, w_in_ref[...])
    q_lat = proj[:, POOL_W:POOL_W + q_lora]
    kv_lat = proj[:, POOL_W + q_lora:POOL_W + q_lora + kv_lora]
    kr_off = POOL_W + q_lora + kv_lora
    st_ref[:, ST_ZP:ST_KR] = proj[:, 0:POOL_W]
    st_ref[:, ST_KR:ST_Q] = proj[:, kr_off:kr_off + LANES]
    qn = _rms_norm(q_lat, qa_g_ref[...]).astype(BF16)
    st_ref[:, ST_Q:ST_KV] = _dot(qn, qb_ref[...])
    kvn = _rms_norm(kv_lat, kva_g_ref[...]).astype(BF16)
    st_ref[:, ST_KV:ST_WIDTH] = _dot(kvn, kvb_ref[...])


def _odd_pre_tail(st_ref, pos_ref, invf_ref, pool_w_ref, pool_scale_ref, qg_n_ref, qg_r_ref,
                  qg_sw_ref, kg_n_ref, kg_r_ref, kg_sw_ref, c_ref, q_ref, k_ref, v_ref,
                  bufs, seq_tile, tm):
    pool_w = POOL_W
    zbuf_ref, s2buf_ref, s4buf_ref, s8buf_ref = bufs
    zp = st_ref[:, ST_ZP:ST_KR]
    kr = st_ref[:, ST_KR:ST_Q]

    lo, hi = slice(0, LANES), slice(LANES, 2 * LANES)

    for buf in bufs:
        buf[0:POOL_CARRY_ROWS, :] = jnp.where(seq_tile == 0, 0.0, buf[0:POOL_CARRY_ROWS, :])

    body = slice(POOL_CARRY_ROWS, POOL_CARRY_ROWS + tm)
    zbuf_ref[body, :] = zp
    s2 = zp + _shifted(zbuf_ref, 1, tm, slice(0, pool_w))
    s2buf_ref[body, :] = s2
    s4 = s2 + _shifted(s2buf_ref, 2, tm, slice(0, pool_w))
    s4buf_ref[body, :] = s4[:, hi]
    s8 = s4[:, hi] + _shifted(s4buf_ref, 4, tm, lo)
    s8buf_ref[body, :] = s8
    s16 = s8 + _shifted(s8buf_ref, 8, tm, lo)
    for buf in bufs:
        buf[0:POOL_CARRY_ROWS, :] = buf[tm:tm + POOL_CARRY_ROWS, :]
    lane = lax.broadcasted_iota(jnp.int32, (tm, LANES), 1)
    first = lane < POOL_GROUP_DIM
    sums = jnp.concatenate([jnp.where(first, s2[:, lo], s4[:, lo]),
                            jnp.where(first, s8, s16)], axis=1)
    head = max(POOL_WINDOWS)
    grp = lax.broadcasted_iota(jnp.int32, (head, pool_w), 1) // POOL_GROUP_DIM
    w_row = jnp.full((head, pool_w), float(POOL_WINDOWS[0]), F32)
    for g, w in enumerate(POOL_WINDOWS[1:], 1):
        w_row = jnp.where(grp >= g, float(w), w_row)
    t1 = (lax.broadcasted_iota(jnp.int32, (head, pool_w), 0) + 1).astype(F32)
    mean = sums * (1.0 / w_row[0:1])
    mean_head = jnp.where(seq_tile == 0, sums[0:head] / jnp.minimum(t1, w_row), mean[0:head])
    mean = jnp.concatenate([mean_head, mean[head:]], axis=0)
    pooled = (mean - zp).astype(BF16)
    c_ref[...] = (_dot(pooled, pool_w_ref[...]) * pool_scale_ref[...]).astype(BF16)

    cosv, sinv = _rope_tables(pos_ref, invf_ref)

    def rope(pair, gain_cos, gain_sin):
        return pair * gain_cos + pltpu.roll(pair, QK_ROPE, 1) * gain_sin

    k_const = math.sqrt(QK_DIM)
    q_const = k_const * QK_DIM ** -0.5 * math.log2(math.e)
    ss_eps = QK_DIM * EPS
    rope_w = MLA_HEADS * LANES
    qg_n = qg_n_ref[...] * q_const
    kg_n = kg_n_ref[...] * k_const
    q_gc, q_gs = (qg_r_ref[...] * q_const) * cosv, (qg_sw_ref[...] * q_const) * sinv
    kr_ss = 0.5 * jnp.sum(kr * kr, axis=-1, keepdims=True) + ss_eps
    kr_rot = rope(kr, (kg_r_ref[...] * k_const) * cosv, (kg_sw_ref[...] * k_const) * sinv)
    for hd in range(MLA_HEADS):
        q0 = ST_Q + hd * LANES
        q_n = st_ref[:, q0:q0 + LANES]
        q_pair = st_ref[:, q0 + rope_w:q0 + rope_w + LANES]
        ss = jnp.sum(q_n * q_n + 0.5 * (q_pair * q_pair), axis=-1, keepdims=True)
        rinv = lax.rsqrt(ss + ss_eps)
        q_ref[hd, :, 0:QK_NOPE] = (q_n * qg_n * rinv).astype(BF16)
        q_ref[hd, :, QK_NOPE:] = (rope(q_pair, q_gc, q_gs) * rinv).astype(BF16)

        k0 = ST_KV + hd * (QK_NOPE + V_DIM)
        k_n = st_ref[:, k0:k0 + QK_NOPE]
        rinv = lax.rsqrt(jnp.sum(k_n * k_n, axis=-1, keepdims=True) + kr_ss)
        k_ref[hd, :, 0:QK_NOPE] = (k_n * kg_n * rinv).astype(BF16)
        k_ref[hd, :, QK_NOPE:] = (kr_rot * rinv).astype(BF16)
        v_ref[hd] = st_ref[:, k0 + QK_NOPE:k0 + QK_NOPE + V_DIM].astype(BF16)


def _odd_pre_kernel(x_ref, pos_ref, invf_ref, g_ref, w_in_ref, pool_w_ref, pool_scale_ref,
                    qa_g_ref, qb_ref, kva_g_ref, kvb_ref, qg_n_ref, qg_r_ref, qg_sw_ref,
                    kg_n_ref, kg_r_ref, kg_sw_ref,
                    c_ref, q_ref, k_ref, v_ref, st_ref, zbuf_ref, s2buf_ref, s4buf_ref, s8buf_ref,
                    *, tm, seq_tiles):
    t = pl.program_id(0)
    seq_tile = lax.rem(jnp.maximum(t - 1, 0), seq_tiles)
    bufs = (zbuf_ref, s2buf_ref, s4buf_ref, s8buf_ref)

    @pl.when(t == 0)
    def _():
        st_ref[1] = jnp.zeros(st_ref.shape[1:], F32)
        for buf in bufs:
            buf[0:POOL_CARRY_ROWS, :] = jnp.zeros((POOL_CARRY_ROWS, buf.shape[1]), F32)

    def step(slot):
        _odd_pre_matmuls(x_ref, g_ref, w_in_ref, qa_g_ref, qb_ref, kva_g_ref, kvb_ref,
                         st_ref.at[slot])
        _odd_pre_tail(st_ref.at[1 - slot], pos_ref, invf_ref, pool_w_ref, pool_scale_ref,
                      qg_n_ref, qg_r_ref, qg_sw_ref, kg_n_ref, kg_r_ref, kg_sw_ref,
                      c_ref, q_ref, k_ref, v_ref, bufs, seq_tile, tm)

    for slot in range(2):
        pl.when(lax.rem(t, 2) == slot)(functools.partial(step, slot))


def _odd_pre(x, positions, g, w_in, pool_w, pool_scale, qa_g, qb, kva_g, kvb,
             qg_n, qg_r, qg_sw, kg_n, kg_r, kg_sw, *, tm):
    b, s, d = x.shape
    pool_width = pool_w.shape[0]
    qk_pad = QK_NOPE + LANES
    groups = LANES // (QK_ROPE // 2)
    seq_tiles = s // tm
    n_tiles = b * seq_tiles
    pos = positions.reshape(b, seq_tiles, groups, tm // groups).transpose(0, 1, 3, 2)
    inv_freq = ROPE_THETA ** (-jnp.arange(0, QK_ROPE, 2, dtype=F32) / QK_ROPE)
    invf = jnp.tile(inv_freq, groups).reshape(1, LANES)

    lead, trail = _skewed_tile_maps(n_tiles, seq_tiles)

    def head_map(t):
        i, j = trail(t)
        return i, 0, j, 0

    consts = [invf, g, w_in, pool_w, pool_scale, qa_g, qb, kva_g, kvb,
              qg_n, qg_r, qg_sw, kg_n, kg_r, kg_sw]
    head_spec = lambda w: pl.BlockSpec((None, MLA_HEADS, tm, w), head_map)
    carry = lambda w: pltpu.VMEM((tm + POOL_CARRY_ROWS, w), F32)
    return pl.pallas_call(
        functools.partial(_odd_pre_kernel, tm=tm, seq_tiles=seq_tiles),
        grid=(n_tiles + 1,),
        in_specs=[pl.BlockSpec((None, tm, d), lambda t: (*lead(t), 0)),
                  pl.BlockSpec((None, None, tm // groups, groups), lambda t: (*trail(t), 0, 0))]
                 + [_full_spec(c) for c in consts],
        out_specs=[pl.BlockSpec((None, tm, pool_width), lambda t: (*trail(t), 0)),
                   head_spec(qk_pad), head_spec(qk_pad), head_spec(V_DIM)],
        out_shape=[jax.ShapeDtypeStruct((b, s, pool_width), BF16),
                   jax.ShapeDtypeStruct((b, MLA_HEADS, s, qk_pad), BF16),
                   jax.ShapeDtypeStruct((b, MLA_HEADS, s, qk_pad), BF16),
                   jax.ShapeDtypeStruct((b, MLA_HEADS, s, V_DIM), BF16)],
        scratch_shapes=[pltpu.VMEM((2, tm, ST_WIDTH), F32),
                        carry(pool_width), carry(pool_width), carry(LANES), carry(LANES)],
        compiler_params=_params(1),
        name="odd_pre",
    )(x, pos, *consts)


def _attn_kernel(q_ref, k_ref, v_ref, o_ref, *, tq, tk, hp):
    s_len = q_ref.shape[1]
    masks = {}

    def causal_mask(nk, offset):
        if (nk, offset) not in masks:
            row = lax.broadcasted_iota(jnp.int32, (tq, nk), 0)
            col = lax.broadcasted_iota(jnp.int32, (tq, nk), 1)
            masks[(nk, offset)] = row + offset >= col
        return masks[(nk, offset)]

    def tile(hd, q0, k0, nk, state):
        q = q_ref[hd, q0:q0 + tq, :]
        k = k_ref[hd, k0:k0 + nk, :]
        v = v_ref[hd, k0:k0 + nk, :]
        sc = lax.dot_general(q, k, (((1,), (1,)), ((), ())), preferred_element_type=F32)
        if k0 + nk - 1 > q0:
            sc = jnp.where(causal_mask(nk, q0 - k0), sc, MASK_VALUE)
        cols = [sc[:, c * LANES:(c + 1) * LANES] for c in range(nk // LANES)]
        rmax = jnp.max(functools.reduce(jnp.maximum, cols), axis=-1, keepdims=True)
        if state is None:
            m_new = jnp.broadcast_to(rmax, (tq, LANES))
            ps = [jnp.exp2(c - m_new) for c in cols]
            return m_new, functools.reduce(jnp.add, ps), _dot(_bf16_cat(ps), v)
        m_prev, l_prev, acc_prev = state
        m_new = jnp.maximum(m_prev, rmax)
        alpha = jnp.exp2(m_prev - m_new)
        ps = [jnp.exp2(c - m_new) for c in cols]
        return (m_new, alpha * l_prev + functools.reduce(jnp.add, ps),
                alpha * acc_prev + _dot(_bf16_cat(ps), v))

    for qi in range(s_len // tq):
        for hd in range(hp):
            q0 = qi * tq
            state = None
            for k0 in range(0, q0 + tq, tk):
                state = tile(hd, q0, k0, min(tk, q0 + tq - k0), state)
            _, l_part, acc = state
            l = jnp.sum(l_part, axis=-1, keepdims=True)
            o_ref[qi * tq:(qi + 1) * tq, hd * V_DIM:(hd + 1) * V_DIM] = (acc / l).astype(BF16)


def _bf16_cat(cols):
    return jnp.concatenate(cols, axis=1).astype(BF16)


def _attention(q, k, v, *, tq, tk, hp):
    b, nh, s, qk_pad = q.shape
    qk_spec = pl.BlockSpec((None, hp, s, qk_pad), lambda i, j: (i, j, 0, 0))
    return pl.pallas_call(
        functools.partial(_attn_kernel, tq=tq, tk=tk, hp=hp),
        grid=(b, nh // hp),
        in_specs=[qk_spec, qk_spec,
                  pl.BlockSpec((None, hp, s, V_DIM), lambda i, j: (i, j, 0, 0))],
        out_specs=pl.BlockSpec((None, s, hp * V_DIM), lambda i, j: (i, 0, j)),
        out_shape=jax.ShapeDtypeStruct((b, s, nh * V_DIM), BF16),
        compiler_params=_params(),
        name="mla_attention",
    )(q, k, v)


def _pad_lanes(a, width):
    return jnp.pad(a, [(0, 0)] * (a.ndim - 1) + [(0, width - a.shape[-1])])


def _swap_halves(a, *, negate_first):
    a1, a2 = jnp.split(a, 2, axis=-1)
    return jnp.concatenate([-a2 if negate_first else a2, a1], axis=-1)


def _rope_pair(w):
    return jnp.concatenate([w, _swap_halves(w, negate_first=True)], axis=-1)


class _TileConfig(NamedTuple):
    tm_even: int
    tm_odd: int
    tm_ffn: int
    ff_chunk: int
    tq: int
    tk: int
    heads_per_step: int


def _tile_config(seq_len):
    cfg = _TileConfig(tm_even=1024, tm_odd=512, tm_ffn=1024, ff_chunk=1 * MXU_WIDTH,
                      tq=256, tk=256, heads_per_step=3)
    assert all(seq_len % t == 0 for t in (cfg.tm_even, cfg.tm_odd, cfg.tm_ffn, cfg.tq))
    assert cfg.tk % cfg.tq == 0
    assert MLA_HEADS % cfg.heads_per_step == 0 and cfg.tm_even % SG_CHUNK == 0
    return cfg


def kernel(x, positions, mix_norm, ffn_norm, even_w_in, sg_ln_g, sg_w_s, sg_b_s, sc_conv_w,
           even_w_out, odd_w_in, pool_w, pool_scale, q_a_norm, q_b, kv_a_norm, kv_b, q_norm,
           k_norm, odd_w_out, ffn_w_gate, ffn_w_up, ffn_w_down):
    depth = mix_norm.shape[0]
    cfg = _tile_config(x.shape[1])
    row = lambda a: a.reshape(1, -1)

    for layer in range(depth):
        i = layer // 2
        if layer % 2 == 0:
            mix = _even_mixer(
                x, row(mix_norm[layer]), even_w_in[i].astype(BF16), row(sg_ln_g[i]),
                sg_w_s[i], sg_b_s[i].T, sc_conv_w[i], tm=cfg.tm_even)
            mixes, w_out = [mix], even_w_out
        else:
            pool_width = pool_scale.shape[-1]
            n_groups = pool_w.shape[1]
            gd = pool_w.shape[2]
            w_bd = jnp.zeros((pool_width, pool_width), F32)
            for gidx in range(n_groups):
                w_bd = w_bd.at[gidx * gd:(gidx + 1) * gd, gidx * gd:(gidx + 1) * gd].set(pool_w[i, gidx])
            q_lora = q_b.shape[1]
            qb3 = q_b[i].reshape(q_lora, MLA_HEADS, QK_DIM)
            qb_nope = qb3[:, :, :QK_NOPE].reshape(q_lora, MLA_HEADS * QK_NOPE)
            qb_pair = _rope_pair(qb3[:, :, QK_NOPE:]).reshape(q_lora, MLA_HEADS * LANES)
            qb = jnp.concatenate([qb_nope, qb_pair], axis=1).astype(BF16)
            w_in = jnp.concatenate(
                [odd_w_in[i][:, :-QK_ROPE], _rope_pair(odd_w_in[i][:, -QK_ROPE:])],
                axis=1).astype(BF16)
            gain_rows = []
            for gn in (q_norm[i], k_norm[i]):
                g_rope = row(gn[QK_NOPE:])
                gain_rows += [row(gn[:QK_NOPE]), _pad_lanes(g_rope, LANES),
                              _pad_lanes(_swap_halves(g_rope, negate_first=False), LANES)]
            c_out, q, k, v = _odd_pre(
                x, positions, row(mix_norm[layer]), w_in, w_bd.astype(BF16), row(pool_scale[i]),
                row(q_a_norm[i]), qb, row(kv_a_norm[i]), kv_b[i].astype(BF16), *gain_rows,
                tm=cfg.tm_odd)
            d_out = _attention(q, k, v, tq=cfg.tq, tk=cfg.tk, hp=cfg.heads_per_step)
            mixes, w_out = [c_out, d_out], odd_w_out
        x = _post(x, mixes, w_out, i, row(ffn_norm[layer]), ffn_w_gate, ffn_w_up, ffn_w_down,
                  layer, tm=cfg.tm_ffn, ff_chunk=cfg.ff_chunk)
    return x
```

```python
import functools
import math
from typing import NamedTuple

import jax
import jax.numpy as jnp
from jax import lax
from jax.experimental import pallas as pl
from jax.experimental.pallas import tpu as pltpu

F32 = jnp.float32
BF16 = jnp.bfloat16

EPS = 1e-6
MASK_VALUE = -1e30
LANES = 128
MXU_WIDTH = 256
SG_CHUNK = 128
SG_HEADS = 4
CONV_WIDTH = 3
POOL_WINDOWS = (2, 4, 8, 16)
POOL_GROUP_DIM = 64
MLA_HEADS = 6
QK_NOPE = 128
QK_ROPE = 64
QK_DIM = QK_NOPE + QK_ROPE
V_DIM = 128
ROPE_THETA = 10000.0
CARRY_ROWS = 8
POOL_CARRY_ROWS = 16
ROW_SUB = 256
VMEM_LIMIT_BYTES = 58 * 1024 * 1024


def _rms_norm(x, g):
    ms = jnp.mean(x * x, axis=-1, keepdims=True)
    return x * lax.rsqrt(ms + EPS) * g


def _gelu(x):
    return 0.5 * x * (1.0 + lax.erf(x * math.sqrt(0.5)))


def _dot(a, b):
    return jnp.dot(a, b, preferred_element_type=F32)


def _full_spec(arr):
    nd = arr.ndim
    return pl.BlockSpec(arr.shape, lambda *_: (0,) * nd)


def _params(grid_rank=2):
    return pltpu.CompilerParams(
        dimension_semantics=("arbitrary",) * grid_rank,
        vmem_limit_bytes=VMEM_LIMIT_BYTES)


def _even_mixer_kernel(x_ref, g_ref, w_in_ref, ln_g_ref, w_s_ref, b_st_ref,
                       conv_w_ref, mix_ref, zbuf_ref, *, tm, sub):
    sg_w = SG_HEADS * LANES
    sc_w = zbuf_ref.shape[1]

    @pl.when(pl.program_id(1) == 0)
    def _():
        zbuf_ref[0:CARRY_ROWS, :] = jnp.zeros((CARRY_ROWS, sc_w), F32)

    row = lax.broadcasted_iota(jnp.int32, (SG_CHUNK, SG_CHUNK), 0)
    col = lax.broadcasted_iota(jnp.int32, (SG_CHUNK, SG_CHUNK), 1)
    causal = row >= col
    ws = [jnp.where(causal, w_s_ref[hd], 0.0).astype(BF16) for hd in range(SG_HEADS)]

    for r0 in range(0, tm, sub):
        h = _rms_norm(x_ref[r0:r0 + sub, :], g_ref[...]).astype(BF16)
        proj = _dot(h, w_in_ref[...])
        u = _gelu(proj[:, 0:sg_w])
        v = _gelu(proj[:, sg_w:2 * sg_w])
        b_gate = proj[:, 2 * sg_w:2 * sg_w + sc_w]
        c_gate = proj[:, 2 * sg_w + sc_w:2 * sg_w + 2 * sc_w]
        hv = proj[:, 2 * sg_w + 2 * sc_w:]

        for hd in range(SG_HEADS):
            cs = slice(hd * LANES, (hd + 1) * LANES)
            vh = v[:, cs]
            mu = jnp.mean(vh, axis=-1, keepdims=True)
            xc = vh - mu
            var = jnp.mean(xc * xc, axis=-1, keepdims=True)
            vn = (xc * lax.rsqrt(var + EPS) * ln_g_ref[:, cs]).astype(BF16)
            bias = b_st_ref[:, hd:hd + 1]
            for c in range(sub // SG_CHUNK):
                rs = slice(c * SG_CHUNK, (c + 1) * SG_CHUNK)
                mixed = _dot(ws[hd], vn[rs]) + bias
                mix_ref[r0 + c * SG_CHUNK:r0 + (c + 1) * SG_CHUNK, cs] = (
                    u[rs, cs] * mixed).astype(BF16)

        z = c_gate * hv
        z0 = CARRY_ROWS + r0
        zbuf_ref[z0:z0 + sub, :] = z
        y = conv_w_ref[CONV_WIDTH - 1:CONV_WIDTH, :] * z
        for k in range(CONV_WIDTH - 1):
            shift = CONV_WIDTH - 1 - k
            y = y + conv_w_ref[k:k + 1, :] * zbuf_ref[z0 - shift:z0 - shift + sub, :]
        mix_ref[r0:r0 + sub, sg_w:] = (b_gate * y).astype(BF16)
    zbuf_ref[0:CARRY_ROWS, :] = zbuf_ref[tm:tm + CARRY_ROWS, :]


def _skewed_tile_maps(n_tiles, seq_tiles):
    def lead(t):
        tt = jnp.minimum(t, n_tiles - 1)
        return tt // seq_tiles, lax.rem(tt, seq_tiles)

    def trail(t):
        tt = jnp.maximum(t - 1, 0)
        return tt // seq_tiles, lax.rem(tt, seq_tiles)

    return lead, trail


def _even_mixer(x, g, w_in, ln_g, w_s, b_st, conv_w, *, tm, sub):
    b, s, d = x.shape
    sc_w = conv_w.shape[1]
    mix_w = SG_HEADS * LANES + sc_w
    tok = pl.BlockSpec((None, tm, d), lambda i, j: (i, j, 0))
    return pl.pallas_call(
        functools.partial(_even_mixer_kernel, tm=tm, sub=sub),
        grid=(b, s // tm),
        in_specs=[tok, _full_spec(g), _full_spec(w_in), _full_spec(ln_g),
                  _full_spec(w_s), _full_spec(b_st), _full_spec(conv_w)],
        out_specs=pl.BlockSpec((None, tm, mix_w), lambda i, j: (i, j, 0)),
        out_shape=jax.ShapeDtypeStruct((b, s, mix_w), BF16),
        scratch_shapes=[pltpu.VMEM((tm + CARRY_ROWS, sc_w), F32)],
        compiler_params=_params(),
        name="even_mixer",
    )(x, g, w_in, ln_g, w_s, b_st, conv_w)


W_CAST_STEPS = 16


def _post_kernel(*refs, n_mix, ff_chunk):
    x_ref = refs[0]
    mix_refs = refs[1:1 + n_mix]
    (wo_ref, g_ref, wg_ref, wu_ref, wd_ref, o_ref,
     wo_s, wg_s, wu_s, wd_s) = refs[1 + n_mix:]
    step = pl.program_id(0)

    @pl.when(step < W_CAST_STEPS)
    def _():
        for src, dst in ((wo_ref, wo_s), (wg_ref, wg_s), (wu_ref, wu_s), (wd_ref, wd_s)):
            rows = src.shape[0]
            dst[pl.ds(pl.multiple_of(step * rows, rows), rows), :] = src[...].astype(BF16)

    @pl.when(step >= W_CAST_STEPS)
    def _():
        mix = [m_ref[...] for m_ref in mix_refs]
        mix = mix[0] if n_mix == 1 else jnp.concatenate(mix, axis=1)
        x1 = x_ref[...] + _dot(mix, wo_s[...])
        h = _rms_norm(x1, g_ref[...]).astype(BF16)
        acc = x1
        d_ff = wg_s.shape[1]
        for c0 in range(0, d_ff, ff_chunk):
            cs = slice(c0, min(c0 + ff_chunk, d_ff))
            gate = _dot(h, wg_s[:, cs])
            up = _dot(h, wu_s[:, cs])
            act = (gate / (1.0 + jnp.exp(-gate)) * up).astype(BF16)
            acc = acc + _dot(act, wd_s[cs, :])
        o_ref[...] = acc


def _post(x, mixes, w_out, out_layer, g, wg, wu, wd, layer, *, tm, ff_chunk):
    b, s, d = x.shape
    seq_tiles = s // tm

    def tile(t):
        tt = jnp.maximum(t - W_CAST_STEPS, 0)
        return tt // seq_tiles, lax.rem(tt, seq_tiles)

    def chunk_spec(w, lyr):
        rows = w.shape[1] // W_CAST_STEPS
        return pl.BlockSpec((None, rows, w.shape[2]),
                            lambda t: (lyr, jnp.minimum(t, W_CAST_STEPS - 1), 0))

    tok = pl.BlockSpec((None, tm, d), lambda t: (*tile(t), 0))
    mix_specs = [pl.BlockSpec((None, tm, m.shape[2]), lambda t: (*tile(t), 0)) for m in mixes]
    return pl.pallas_call(
        functools.partial(_post_kernel, n_mix=len(mixes), ff_chunk=ff_chunk),
        grid=(W_CAST_STEPS + b * seq_tiles,),
        in_specs=[tok] + mix_specs
                 + [chunk_spec(w_out, out_layer), _full_spec(g), chunk_spec(wg, layer),
                    chunk_spec(wu, layer), chunk_spec(wd, layer)],
        out_specs=tok,
        out_shape=jax.ShapeDtypeStruct(x.shape, x.dtype),
        scratch_shapes=[pltpu.VMEM(w.shape[1:], BF16) for w in (w_out, wg, wu, wd)],
        compiler_params=_params(1),
        name="outproj_ffn",
    )(x, *mixes, w_out, g, wg, wu, wd)


def _rope_tables(pos_ref, invf_ref):
    half = QK_ROPE // 2
    groups = LANES // half
    rows = pos_ref.shape[0]
    pos = pos_ref[...].astype(F32)
    lane = lax.broadcasted_iota(jnp.int32, (rows, LANES), 1)
    p = pos[:, 0:1]
    for grp in range(1, groups):
        p = jnp.where(lane >= grp * half, pos[:, grp:grp + 1], p)
    ang = p * invf_ref[...]
    tables = []
    for packed in (jnp.cos(ang), jnp.sin(ang)):
        quarters = []
        for grp in range(groups):
            r = packed if grp == 0 else pltpu.roll(packed, LANES - grp * half, 1)
            quarters.append(jnp.where(lane < half, r, pltpu.roll(r, half, 1)))
        tables.append(jnp.concatenate(quarters, axis=0))
    return tables


def _shifted(buf_ref, shift, tm, cs):
    return buf_ref[POOL_CARRY_ROWS - shift:POOL_CARRY_ROWS - shift + tm, cs]


POOL_W = len(POOL_WINDOWS) * POOL_GROUP_DIM
ST_ZP = 0
ST_KR = ST_ZP + POOL_W
ST_Q = ST_KR + LANES
ST_KV = ST_Q + 2 * MLA_HEADS * LANES
ST_WIDTH = ST_KV + MLA_HEADS * (QK_NOPE + V_DIM)


def _odd_pre_matmuls(x_ref, g_ref, w_in_ref, qa_g_ref, qb_ref, kva_g_ref, kvb_ref, st_ref):
    q_lora = qa_g_ref.shape[1]
    kv_lora = kva_g_ref.shape[1]
    h = _rms_norm(x_ref[...], g_ref[...]).astype(BF16)
    proj = _dot(h, w_in_ref[...])
    q_lat = proj[:, POOL_W:POOL_W + q_lora]
    kv_lat = proj[:, POOL_W + q_lora:POOL_W + q_lora + kv_lora]
    kr_off = POOL_W + q_lora + kv_lora
    st_ref[:, ST_ZP:ST_KR] = proj[:, 0:POOL_W]
    st_ref[:, ST_KR:ST_Q] = proj[:, kr_off:kr_off + LANES]
    qn = _rms_norm(q_lat, qa_g_ref[...]).astype(BF16)
    st_ref[:, ST_Q:ST_KV] = _dot(qn, qb_ref[...])
    kvn = _rms_norm(kv_lat, kva_g_ref[...]).astype(BF16)
    st_ref[:, ST_KV:ST_WIDTH] = _dot(kvn, kvb_ref[...])


def _odd_pre_tail(st_ref, pos_ref, invf_ref, pool_w_ref, pool_scale_ref, qg_n_ref, qg_r_ref,
                  qg_sw_ref, kg_n_ref, kg_r_ref, kg_sw_ref, c_ref, q_ref, k_ref, v_ref,
                  bufs, seq_tile, tm):
    pool_w = POOL_W
    zbuf_ref, s2buf_ref, s4buf_ref, s8buf_ref = bufs
    zp = st_ref[:, ST_ZP:ST_KR]
    kr = st_ref[:, ST_KR:ST_Q]

    lo, hi = slice(0, LANES), slice(LANES, 2 * LANES)

    for buf in bufs:
        buf[0:POOL_CARRY_ROWS, :] = jnp.where(seq_tile == 0, 0.0, buf[0:POOL_CARRY_ROWS, :])

    body = slice(POOL_CARRY_ROWS, POOL_CARRY_ROWS + tm)
    zbuf_ref[body, :] = zp
    s2 = zp + _shifted(zbuf_ref, 1, tm, slice(0, pool_w))
    s2buf_ref[body, :] = s2
    s4 = s2 + _shifted(s2buf_ref, 2, tm, slice(0, pool_w))
    s4buf_ref[body, :] = s4[:, hi]
    s8 = s4[:, hi] + _shifted(s4buf_ref, 4, tm, lo)
    s8buf_ref[body, :] = s8
    s16 = s8 + _shifted(s8buf_ref, 8, tm, lo)
    for buf in bufs:
        buf[0:POOL_CARRY_ROWS, :] = buf[tm:tm + POOL_CARRY_ROWS, :]
    lane = lax.broadcasted_iota(jnp.int32, (tm, LANES), 1)
    first = lane < POOL_GROUP_DIM
    sums = jnp.concatenate([jnp.where(first, s2[:, lo], s4[:, lo]),
                            jnp.where(first, s8, s16)], axis=1)
    head = max(POOL_WINDOWS)
    grp = lax.broadcasted_iota(jnp.int32, (head, pool_w), 1) // POOL_GROUP_DIM
    w_row = jnp.full((head, pool_w), float(POOL_WINDOWS[0]), F32)
    for g, w in enumerate(POOL_WINDOWS[1:], 1):
        w_row = jnp.where(grp >= g, float(w), w_row)
    t1 = (lax.broadcasted_iota(jnp.int32, (head, pool_w), 0) + 1).astype(F32)
    mean = sums * (1.0 / w_row[0:1])
    mean_head = jnp.where(seq_tile == 0, sums[0:head] / jnp.minimum(t1, w_row), mean[0:head])
    mean = jnp.concatenate([mean_head, mean[head:]], axis=0)
    pooled = (mean - zp).astype(BF16)
    c_ref[...] = (_dot(pooled, pool_w_ref[...]) * pool_scale_ref[...]).astype(BF16)

    cosv, sinv = _rope_tables(pos_ref, invf_ref)

    def rope(pair, gain_cos, gain_sin):
        return pair * gain_cos + pltpu.roll(pair, QK_ROPE, 1) * gain_sin

    k_const = math.sqrt(QK_DIM)
    q_const = k_const * QK_DIM ** -0.5 * math.log2(math.e)
    ss_eps = QK_DIM * EPS
    rope_w = MLA_HEADS * LANES
    qg_n = qg_n_ref[...] * q_const
    kg_n = kg_n_ref[...] * k_const
    q_gc, q_gs = (qg_r_ref[...] * q_const) * cosv, (qg_sw_ref[...] * q_const) * sinv
    kr_ss = 0.5 * jnp.sum(kr * kr, axis=-1, keepdims=True) + ss_eps
    kr_rot = rope(kr, (kg_r_ref[...] * k_const) * cosv, (kg_sw_ref[...] * k_const) * sinv)
    for hd in range(MLA_HEADS):
        q0 = ST_Q + hd * LANES
        q_n = st_ref[:, q0:q0 + LANES]
        q_pair = st_ref[:, q0 + rope_w:q0 + rope_w + LANES]
        ss = jnp.sum(q_n * q_n + 0.5 * (q_pair * q_pair), axis=-1, keepdims=True)
        rinv = lax.rsqrt(ss + ss_eps)
        q_ref[hd, :, 0:QK_NOPE] = (q_n * qg_n * rinv).astype(BF16)
        q_ref[hd, :, QK_NOPE:] = (rope(q_pair, q_gc, q_gs) * rinv).astype(BF16)

        k0 = ST_KV + hd * (QK_NOPE + V_DIM)
        k_n = st_ref[:, k0:k0 + QK_NOPE]
        rinv = lax.rsqrt(jnp.sum(k_n * k_n, axis=-1, keepdims=True) + kr_ss)
        k_ref[hd, :, 0:QK_NOPE] = (k_n * kg_n * rinv).astype(BF16)
        k_ref[hd, :, QK_NOPE:] = (kr_rot * rinv).astype(BF16)
        v_ref[hd] = st_ref[:, k0 + QK_NOPE:k0 + QK_NOPE + V_DIM].astype(BF16)


def _odd_pre_kernel(x_ref, pos_ref, invf_ref, g_ref, w_in_ref, pool_w_ref, pool_scale_ref,
                    qa_g_ref, qb_ref, kva_g_ref, kvb_ref, qg_n_ref, qg_r_ref, qg_sw_ref,
                    kg_n_ref, kg_r_ref, kg_sw_ref,
                    c_ref, q_ref, k_ref, v_ref, st_ref, zbuf_ref, s2buf_ref, s4buf_ref, s8buf_ref,
                    *, tm, seq_tiles):
    t = pl.program_id(0)
    seq_tile = lax.rem(jnp.maximum(t - 1, 0), seq_tiles)
    bufs = (zbuf_ref, s2buf_ref, s4buf_ref, s8buf_ref)

    @pl.when(t == 0)
    def _():
        st_ref[1] = jnp.zeros(st_ref.shape[1:], F32)
        for buf in bufs:
            buf[0:POOL_CARRY_ROWS, :] = jnp.zeros((POOL_CARRY_ROWS, buf.shape[1]), F32)

    def step(slot):
        _odd_pre_matmuls(x_ref, g_ref, w_in_ref, qa_g_ref, qb_ref, kva_g_ref, kvb_ref,
                         st_ref.at[slot])
        _odd_pre_tail(st_ref.at[1 - slot], pos_ref, invf_ref, pool_w_ref, pool_scale_ref,
                      qg_n_ref, qg_r_ref, qg_sw_ref, kg_n_ref, kg_r_ref, kg_sw_ref,
                      c_ref, q_ref, k_ref, v_ref, bufs, seq_tile, tm)

    for slot in range(2):
        pl.when(lax.rem(t, 2) == slot)(functools.partial(step, slot))


def _odd_pre(x, positions, g, w_in, pool_w, pool_scale, qa_g, qb, kva_g, kvb,
             qg_n, qg_r, qg_sw, kg_n, kg_r, kg_sw, *, tm):
    b, s, d = x.shape
    pool_width = pool_w.shape[0]
    qk_pad = QK_NOPE + LANES
    groups = LANES // (QK_ROPE // 2)
    seq_tiles = s // tm
    n_tiles = b * seq_tiles
    pos = positions.reshape(b, seq_tiles, groups, tm // groups).transpose(0, 1, 3, 2)
    inv_freq = ROPE_THETA ** (-jnp.arange(0, QK_ROPE, 2, dtype=F32) / QK_ROPE)
    invf = jnp.tile(inv_freq, groups).reshape(1, LANES)

    lead, trail = _skewed_tile_maps(n_tiles, seq_tiles)

    def head_map(t):
        i, j = trail(t)
        return i, 0, j, 0

    consts = [invf, g, w_in, pool_w, pool_scale, qa_g, qb, kva_g, kvb,
              qg_n, qg_r, qg_sw, kg_n, kg_r, kg_sw]
    head_spec = lambda w: pl.BlockSpec((None, MLA_HEADS, tm, w), head_map)
    carry = lambda w: pltpu.VMEM((tm + POOL_CARRY_ROWS, w), F32)
    return pl.pallas_call(
        functools.partial(_odd_pre_kernel, tm=tm, seq_tiles=seq_tiles),
        grid=(n_tiles + 1,),
        in_specs=[pl.BlockSpec((None, tm, d), lambda t: (*lead(t), 0)),
                  pl.BlockSpec((None, None, tm // groups, groups), lambda t: (*trail(t), 0, 0))]
                 + [_full_spec(c) for c in consts],
        out_specs=[pl.BlockSpec((None, tm, pool_width), lambda t: (*trail(t), 0)),
                   head_spec(qk_pad), head_spec(qk_pad), head_spec(V_DIM)],
        out_shape=[jax.ShapeDtypeStruct((b, s, pool_width), BF16),
                   jax.ShapeDtypeStruct((b, MLA_HEADS, s, qk_pad), BF16),
                   jax.ShapeDtypeStruct((b, MLA_HEADS, s, qk_pad), BF16),
                   jax.ShapeDtypeStruct((b, MLA_HEADS, s, V_DIM), BF16)],
        scratch_shapes=[pltpu.VMEM((2, tm, ST_WIDTH), F32),
                        carry(pool_width), carry(pool_width), carry(LANES), carry(LANES)],
        compiler_params=_params(1),
        name="odd_pre",
    )(x, pos, *consts)


def _attn_kernel(q_ref, k_ref, v_ref, o_ref, *, tq, tk, hp):
    s_len = q_ref.shape[1]
    masks = {}

    def causal_mask(nk, offset):
        if (nk, offset) not in masks:
            row = lax.broadcasted_iota(jnp.int32, (tq, nk), 0)
            col = lax.broadcasted_iota(jnp.int32, (tq, nk), 1)
            masks[(nk, offset)] = row + offset >= col
        return masks[(nk, offset)]

    def tile(hd, q0, k0, nk, state):
        q = q_ref[hd, q0:q0 + tq, :]
        k = k_ref[hd, k0:k0 + nk, :]
        v = v_ref[hd, k0:k0 + nk, :]
        sc = lax.dot_general(q, k, (((1,), (1,)), ((), ())), preferred_element_type=F32)
        if k0 + nk - 1 > q0:
            sc = jnp.where(causal_mask(nk, q0 - k0), sc, MASK_VALUE)
        cols = [sc[:, c * LANES:(c + 1) * LANES] for c in range(nk // LANES)]
        rmax = jnp.max(functools.reduce(jnp.maximum, cols), axis=-1, keepdims=True)
        if state is None:
            m_new = jnp.broadcast_to(rmax, (tq, LANES))
            ps = [jnp.exp2(c - m_new) for c in cols]
            return m_new, functools.reduce(jnp.add, ps), _dot(_bf16_cat(ps), v)
        m_prev, l_prev, acc_prev = state
        m_new = jnp.maximum(m_prev, rmax)
        alpha = jnp.exp2(m_prev - m_new)
        ps = [jnp.exp2(c - m_new) for c in cols]
        return (m_new, alpha * l_prev + functools.reduce(jnp.add, ps),
                alpha * acc_prev + _dot(_bf16_cat(ps), v))

    for qi in range(s_len // tq):
        for hd in range(hp):
            q0 = qi * tq
            state = None
            for k0 in range(0, q0 + tq, tk):
                state = tile(hd, q0, k0, min(tk, q0 + tq - k0), state)
            _, l_part, acc = state
            l = jnp.sum(l_part, axis=-1, keepdims=True)
            o_ref[qi * tq:(qi + 1) * tq, hd * V_DIM:(hd + 1) * V_DIM] = (acc / l).astype(BF16)


def _bf16_cat(cols):
    return jnp.concatenate(cols, axis=1).astype(BF16)


def _attention(q, k, v, *, tq, tk, hp):
    b, nh, s, qk_pad = q.shape
    qk_spec = pl.BlockSpec((None, hp, s, qk_pad), lambda i, j: (i, j, 0, 0))
    return pl.pallas_call(
        functools.partial(_attn_kernel, tq=tq, tk=tk, hp=hp),
        grid=(b, nh // hp),
        in_specs=[qk_spec, qk_spec,
                  pl.BlockSpec((None, hp, s, V_DIM), lambda i, j: (i, j, 0, 0))],
        out_specs=pl.BlockSpec((None, s, hp * V_DIM), lambda i, j: (i, 0, j)),
        out_shape=jax.ShapeDtypeStruct((b, s, nh * V_DIM), BF16),
        compiler_params=_params(),
        name="mla_attention",
    )(q, k, v)


def _pad_lanes(a, width):
    return jnp.pad(a, [(0, 0)] * (a.ndim - 1) + [(0, width - a.shape[-1])])


def _swap_halves(a, *, negate_first):
    a1, a2 = jnp.split(a, 2, axis=-1)
    return jnp.concatenate([-a2 if negate_first else a2, a1], axis=-1)


def _rope_pair(w):
    return jnp.concatenate([w, _swap_halves(w, negate_first=True)], axis=-1)


class _TileConfig(NamedTuple):
    tm_even: int
    tm_odd: int
    tm_ffn: int
    ff_chunk: int
    tq: int
    tk: int
    heads_per_step: int


def _tile_config(seq_len):
    cfg = _TileConfig(tm_even=1024, tm_odd=512, tm_ffn=1024, ff_chunk=1 * MXU_WIDTH,
                      tq=256, tk=256, heads_per_step=3)
    assert all(seq_len % t == 0 for t in (cfg.tm_even, cfg.tm_odd, cfg.tm_ffn, cfg.tq))
    assert cfg.tk % cfg.tq == 0
    assert MLA_HEADS % cfg.heads_per_step == 0 and cfg.tm_even % SG_CHUNK == 0
    return cfg


def kernel(x, positions, mix_norm, ffn_norm, even_w_in, sg_ln_g, sg_w_s, sg_b_s, sc_conv_w,
           even_w_out, odd_w_in, pool_w, pool_scale, q_a_norm, q_b, kv_a_norm, kv_b, q_norm,
           k_norm, odd_w_out, ffn_w_gate, ffn_w_up, ffn_w_down):
    depth = mix_norm.shape[0]
    cfg = _tile_config(x.shape[1])
    row = lambda a: a.reshape(1, -1)

    for layer in range(depth):
        i = layer // 2
        if layer % 2 == 0:
            mix = _even_mixer(
                x, row(mix_norm[layer]), even_w_in[i].astype(BF16), row(sg_ln_g[i]),
                sg_w_s[i], sg_b_s[i].T, sc_conv_w[i], tm=cfg.tm_even, sub=ROW_SUB)
            mixes, w_out = [mix], even_w_out
        else:
            pool_width = pool_scale.shape[-1]
            n_groups = pool_w.shape[1]
            gd = pool_w.shape[2]
            w_bd = jnp.zeros((pool_width, pool_width), F32)
            for gidx in range(n_groups):
                w_bd = w_bd.at[gidx * gd:(gidx + 1) * gd, gidx * gd:(gidx + 1) * gd].set(pool_w[i, gidx])
            q_lora = q_b.shape[1]
            qb3 = q_b[i].reshape(q_lora, MLA_HEADS, QK_DIM)
            qb_nope = qb3[:, :, :QK_NOPE].reshape(q_lora, MLA_HEADS * QK_NOPE)
            qb_pair = _rope_pair(qb3[:, :, QK_NOPE:]).reshape(q_lora, MLA_HEADS * LANES)
            qb = jnp.concatenate([qb_nope, qb_pair], axis=1).astype(BF16)
            w_in = jnp.concatenate(
                [odd_w_in[i][:, :-QK_ROPE], _rope_pair(odd_w_in[i][:, -QK_ROPE:])],
                axis=1).astype(BF16)
            gain_rows = []
            for gn in (q_norm[i], k_norm[i]):
                g_rope = row(gn[QK_NOPE:])
                gain_rows += [row(gn[:QK_NOPE]), _pad_lanes(g_rope, LANES),
                              _pad_lanes(_swap_halves(g_rope, negate_first=False), LANES)]
            c_out, q, k, v = _odd_pre(
                x, positions, row(mix_norm[layer]), w_in, w_bd.astype(BF16), row(pool_scale[i]),
                row(q_a_norm[i]), qb, row(kv_a_norm[i]), kv_b[i].astype(BF16), *gain_rows,
                tm=cfg.tm_odd)
            d_out = _attention(q, k, v, tq=cfg.tq, tk=cfg.tk, hp=cfg.heads_per_step)
            mixes, w_out = [c_out, d_out], odd_w_out
        x = _post(x, mixes, w_out, i, row(ffn_norm[layer]), ffn_w_gate, ffn_w_up, ffn_w_down,
                  layer, tm=cfg.tm_ffn, ff_chunk=cfg.ff_chunk)
    return x
```

```python
import functools
import math
from typing import NamedTuple

import jax
import jax.numpy as jnp
from jax import lax
from jax.experimental import pallas as pl
from jax.experimental.pallas import tpu as pltpu

F32 = jnp.float32
BF16 = jnp.bfloat16

EPS = 1e-6
MASK_VALUE = -1e30
LANES = 128
MXU_WIDTH = 256
SG_CHUNK = 128
SG_HEADS = 4
CONV_WIDTH = 3
POOL_WINDOWS = (2, 4, 8, 16)
POOL_GROUP_DIM = 64
MLA_HEADS = 6
QK_NOPE = 128
QK_ROPE = 64
QK_DIM = QK_NOPE + QK_ROPE
V_DIM = 128
ROPE_THETA = 10000.0
CARRY_ROWS = 8
POOL_CARRY_ROWS = 16
VMEM_LIMIT_BYTES = 58 * 1024 * 1024


def _rms_norm(x, g):
    ms = jnp.mean(x * x, axis=-1, keepdims=True)
    return x * lax.rsqrt(ms + EPS) * g


def _gelu(x):
    return 0.5 * x * (1.0 + lax.erf(x * math.sqrt(0.5)))


def _dot(a, b):
    return jnp.dot(a, b, preferred_element_type=F32)


def _full_spec(arr):
    nd = arr.ndim
    return pl.BlockSpec(arr.shape, lambda *_: (0,) * nd)


def _params(grid_rank=2):
    return pltpu.CompilerParams(
        dimension_semantics=("arbitrary",) * grid_rank,
        vmem_limit_bytes=VMEM_LIMIT_BYTES)


def _even_mixer_kernel(x_ref, g_ref, w_in_ref, ln_g_ref, w_s_ref, b_st_ref,
                       conv_w_ref, mix_ref, zbuf_ref, *, tm):
    sg_w = SG_HEADS * LANES
    h = _rms_norm(x_ref[...], g_ref[...]).astype(BF16)
    proj = _dot(h, w_in_ref[...])
    u = _gelu(proj[:, 0:sg_w])
    v = _gelu(proj[:, sg_w:2 * sg_w])
    sc_w = (proj.shape[1] - 2 * sg_w) // 3
    b_gate = proj[:, 2 * sg_w:2 * sg_w + sc_w]
    c_gate = proj[:, 2 * sg_w + sc_w:2 * sg_w + 2 * sc_w]
    hv = proj[:, 2 * sg_w + 2 * sc_w:]

    row = lax.broadcasted_iota(jnp.int32, (SG_CHUNK, SG_CHUNK), 0)
    col = lax.broadcasted_iota(jnp.int32, (SG_CHUNK, SG_CHUNK), 1)
    causal = row >= col
    for hd in range(SG_HEADS):
        cs = slice(hd * LANES, (hd + 1) * LANES)
        vh = v[:, cs]
        mu = jnp.mean(vh, axis=-1, keepdims=True)
        xc = vh - mu
        var = jnp.mean(xc * xc, axis=-1, keepdims=True)
        vn = (xc * lax.rsqrt(var + EPS) * ln_g_ref[:, cs]).astype(BF16)
        w = jnp.where(causal, w_s_ref[hd], 0.0).astype(BF16)
        bias = b_st_ref[:, hd:hd + 1]
        for c in range(tm // SG_CHUNK):
            rs = slice(c * SG_CHUNK, (c + 1) * SG_CHUNK)
            mixed = _dot(w, vn[rs]) + bias
            mix_ref[rs, cs] = (u[rs, cs] * mixed).astype(BF16)

    @pl.when(pl.program_id(1) == 0)
    def _():
        zbuf_ref[0:CARRY_ROWS, :] = jnp.zeros((CARRY_ROWS, sc_w), F32)

    z = c_gate * hv
    zbuf_ref[CARRY_ROWS:CARRY_ROWS + tm, :] = z
    y = conv_w_ref[CONV_WIDTH - 1:CONV_WIDTH, :] * z
    for k in range(CONV_WIDTH - 1):
        shift = CONV_WIDTH - 1 - k
        y = y + conv_w_ref[k:k + 1, :] * zbuf_ref[CARRY_ROWS - shift:CARRY_ROWS - shift + tm, :]
    mix_ref[:, sg_w:] = (b_gate * y).astype(BF16)
    zbuf_ref[0:CARRY_ROWS, :] = zbuf_ref[tm:tm + CARRY_ROWS, :]


def _skewed_tile_maps(n_tiles, seq_tiles):
    def lead(t):
        tt = jnp.minimum(t, n_tiles - 1)
        return tt // seq_tiles, lax.rem(tt, seq_tiles)

    def trail(t):
        tt = jnp.maximum(t - 1, 0)
        return tt // seq_tiles, lax.rem(tt, seq_tiles)

    return lead, trail


def _even_mixer(x, g, w_in, ln_g, w_s, b_st, conv_w, *, tm):
    b, s, d = x.shape
    sc_w = conv_w.shape[1]
    mix_w = SG_HEADS * LANES + sc_w
    tok = pl.BlockSpec((None, tm, d), lambda i, j: (i, j, 0))
    return pl.pallas_call(
        functools.partial(_even_mixer_kernel, tm=tm),
        grid=(b, s // tm),
        in_specs=[tok, _full_spec(g), _full_spec(w_in), _full_spec(ln_g),
                  _full_spec(w_s), _full_spec(b_st), _full_spec(conv_w)],
        out_specs=pl.BlockSpec((None, tm, mix_w), lambda i, j: (i, j, 0)),
        out_shape=jax.ShapeDtypeStruct((b, s, mix_w), BF16),
        scratch_shapes=[pltpu.VMEM((tm + CARRY_ROWS, sc_w), F32)],
        compiler_params=_params(),
        name="even_mixer",
    )(x, g, w_in, ln_g, w_s, b_st, conv_w)


W_CAST_STEPS = 16


def _post_kernel(*refs, n_mix, ff_chunk):
    x_ref = refs[0]
    mix_refs = refs[1:1 + n_mix]
    (wo_ref, g_ref, wg_ref, wu_ref, wd_ref, o_ref,
     wo_s, wg_s, wu_s, wd_s) = refs[1 + n_mix:]
    step = pl.program_id(0)

    @pl.when(step < W_CAST_STEPS)
    def _():
        for src, dst in ((wo_ref, wo_s), (wg_ref, wg_s), (wu_ref, wu_s), (wd_ref, wd_s)):
            rows = src.shape[0]
            dst[pl.ds(pl.multiple_of(step * rows, rows), rows), :] = src[...].astype(BF16)

    @pl.when(step >= W_CAST_STEPS)
    def _():
        mix = [m_ref[...] for m_ref in mix_refs]
        mix = mix[0] if n_mix == 1 else jnp.concatenate(mix, axis=1)
        x1 = x_ref[...] + _dot(mix, wo_s[...])
        h = _rms_norm(x1, g_ref[...]).astype(BF16)
        acc = x1
        d_ff = wg_s.shape[1]
        for c0 in range(0, d_ff, ff_chunk):
            cs = slice(c0, min(c0 + ff_chunk, d_ff))
            gate = _dot(h, wg_s[:, cs])
            up = _dot(h, wu_s[:, cs])
            act = (gate / (1.0 + jnp.exp(-gate)) * up).astype(BF16)
            acc = acc + _dot(act, wd_s[cs, :])
        o_ref[...] = acc


def _post(x, mixes, w_out, out_layer, g, wg, wu, wd, layer, *, tm, ff_chunk):
    b, s, d = x.shape
    seq_tiles = s // tm

    def tile(t):
        tt = jnp.maximum(t - W_CAST_STEPS, 0)
        return tt // seq_tiles, lax.rem(tt, seq_tiles)

    def chunk_spec(w, lyr):
        rows = w.shape[1] // W_CAST_STEPS
        return pl.BlockSpec((None, rows, w.shape[2]),
                            lambda t: (lyr, jnp.minimum(t, W_CAST_STEPS - 1), 0))

    tok = pl.BlockSpec((None, tm, d), lambda t: (*tile(t), 0))
    mix_specs = [pl.BlockSpec((None, tm, m.shape[2]), lambda t: (*tile(t), 0)) for m in mixes]
    return pl.pallas_call(
        functools.partial(_post_kernel, n_mix=len(mixes), ff_chunk=ff_chunk),
        grid=(W_CAST_STEPS + b * seq_tiles,),
        in_specs=[tok] + mix_specs
                 + [chunk_spec(w_out, out_layer), _full_spec(g), chunk_spec(wg, layer),
                    chunk_spec(wu, layer), chunk_spec(wd, layer)],
        out_specs=tok,
        out_shape=jax.ShapeDtypeStruct(x.shape, x.dtype),
        scratch_shapes=[pltpu.VMEM(w.shape[1:], BF16) for w in (w_out, wg, wu, wd)],
        compiler_params=_params(1),
        name="outproj_ffn",
    )(x, *mixes, w_out, g, wg, wu, wd)


def _rope_tables(pos_ref, invf_ref):
    half = QK_ROPE // 2
    groups = LANES // half
    rows = pos_ref.shape[0]
    pos = pos_ref[...].astype(F32)
    lane = lax.broadcasted_iota(jnp.int32, (rows, LANES), 1)
    p = pos[:, 0:1]
    for grp in range(1, groups):
        p = jnp.where(lane >= grp * half, pos[:, grp:grp + 1], p)
    ang = p * invf_ref[...]
    tables = []
    for packed in (jnp.cos(ang), jnp.sin(ang)):
        quarters = []
        for grp in range(groups):
            r = packed if grp == 0 else pltpu.roll(packed, LANES - grp * half, 1)
            quarters.append(jnp.where(lane < half, r, pltpu.roll(r, half, 1)))
        tables.append(jnp.concatenate(quarters, axis=0))
    return tables


def _shifted(buf_ref, shift, tm, cs):
    return buf_ref[POOL_CARRY_ROWS - shift:POOL_CARRY_ROWS - shift + tm, cs]


POOL_W = len(POOL_WINDOWS) * POOL_GROUP_DIM
ST_ZP = 0
ST_KR = ST_ZP + POOL_W
ST_Q = ST_KR + LANES
ST_K = ST_Q + 2 * MLA_HEADS * LANES
ST_WIDTH = ST_K + MLA_HEADS * QK_NOPE


def _odd_pre_matmuls(x_ref, g_ref, w_in_ref, qa_g_ref, qb_ref, kva_g_ref, kvb_ref, st_ref, sv_ref):
    q_lora = qa_g_ref.shape[1]
    kv_lora = kva_g_ref.shape[1]
    h = _rms_norm(x_ref[...], g_ref[...]).astype(BF16)
    proj = _dot(h, w_in_ref[...])
    q_lat = proj[:, POOL_W:POOL_W + q_lora]
    kv_lat = proj[:, POOL_W + q_lora:POOL_W + q_lora + kv_lora]
    kr_off = POOL_W + q_lora + kv_lora
    st_ref[:, ST_ZP:ST_KR] = proj[:, 0:POOL_W]
    st_ref[:, ST_KR:ST_Q] = proj[:, kr_off:kr_off + LANES]
    qn = _rms_norm(q_lat, qa_g_ref[...]).astype(BF16)
    st_ref[:, ST_Q:ST_K] = _dot(qn, qb_ref[...])
    kvn = _rms_norm(kv_lat, kva_g_ref[...]).astype(BF16)
    kv = _dot(kvn, kvb_ref[...])
    for hd in range(MLA_HEADS):
        c0 = hd * (QK_NOPE + V_DIM)
        st_ref[:, ST_K + hd * QK_NOPE:ST_K + (hd + 1) * QK_NOPE] = kv[:, c0:c0 + QK_NOPE]
        sv_ref[:, hd * V_DIM:(hd + 1) * V_DIM] = kv[:, c0 + QK_NOPE:c0 + QK_NOPE + V_DIM].astype(BF16)


def _odd_pre_tail(st_ref, sv_ref, pos_ref, invf_ref, pool_w_ref, pool_scale_ref, qg_n_ref, qg_r_ref,
                  qg_sw_ref, kg_n_ref, kg_r_ref, kg_sw_ref, c_ref, q_ref, k_ref, v_ref,
                  bufs, seq_tile, tm):
    pool_w = POOL_W
    zbuf_ref, s2buf_ref, s4buf_ref, s8buf_ref = bufs
    zp = st_ref[:, ST_ZP:ST_KR]
    kr = st_ref[:, ST_KR:ST_Q]

    lo, hi = slice(0, LANES), slice(LANES, 2 * LANES)

    for buf in bufs:
        buf[0:POOL_CARRY_ROWS, :] = jnp.where(seq_tile == 0, 0.0, buf[0:POOL_CARRY_ROWS, :])

    body = slice(POOL_CARRY_ROWS, POOL_CARRY_ROWS + tm)
    zbuf_ref[body, :] = zp
    s2 = zp + _shifted(zbuf_ref, 1, tm, slice(0, pool_w))
    s2buf_ref[body, :] = s2
    s4 = s2 + _shifted(s2buf_ref, 2, tm, slice(0, pool_w))
    s4buf_ref[body, :] = s4[:, hi]
    s8 = s4[:, hi] + _shifted(s4buf_ref, 4, tm, lo)
    s8buf_ref[body, :] = s8
    s16 = s8 + _shifted(s8buf_ref, 8, tm, lo)
    for buf in bufs:
        buf[0:POOL_CARRY_ROWS, :] = buf[tm:tm + POOL_CARRY_ROWS, :]
    lane = lax.broadcasted_iota(jnp.int32, (tm, LANES), 1)
    first = lane < POOL_GROUP_DIM
    sums = jnp.concatenate([jnp.where(first, s2[:, lo], s4[:, lo]),
                            jnp.where(first, s8, s16)], axis=1)
    head = max(POOL_WINDOWS)
    grp = lax.broadcasted_iota(jnp.int32, (head, pool_w), 1) // POOL_GROUP_DIM
    w_row = jnp.full((head, pool_w), float(POOL_WINDOWS[0]), F32)
    for g, w in enumerate(POOL_WINDOWS[1:], 1):
        w_row = jnp.where(grp >= g, float(w), w_row)
    t1 = (lax.broadcasted_iota(jnp.int32, (head, pool_w), 0) + 1).astype(F32)
    mean = sums * (1.0 / w_row[0:1])
    mean_head = jnp.where(seq_tile == 0, sums[0:head] / jnp.minimum(t1, w_row), mean[0:head])
    mean = jnp.concatenate([mean_head, mean[head:]], axis=0)
    pooled = (mean - zp).astype(BF16)
    c_ref[...] = (_dot(pooled, pool_w_ref[...]) * pool_scale_ref[...]).astype(BF16)

    cosv, sinv = _rope_tables(pos_ref, invf_ref)

    def rope(pair, gain_cos, gain_sin):
        return pair * gain_cos + pltpu.roll(pair, QK_ROPE, 1) * gain_sin

    k_const = math.sqrt(QK_DIM)
    q_const = k_const * QK_DIM ** -0.5 * math.log2(math.e)
    ss_eps = QK_DIM * EPS
    rope_w = MLA_HEADS * LANES
    qg_n = qg_n_ref[...] * q_const
    kg_n = kg_n_ref[...] * k_const
    q_gc, q_gs = (qg_r_ref[...] * q_const) * cosv, (qg_sw_ref[...] * q_const) * sinv
    kr_ss = 0.5 * jnp.sum(kr * kr, axis=-1, keepdims=True) + ss_eps
    kr_rot = rope(kr, (kg_r_ref[...] * k_const) * cosv, (kg_sw_ref[...] * k_const) * sinv)
    for hd in range(MLA_HEADS):
        q0 = ST_Q + hd * LANES
        q_n = st_ref[:, q0:q0 + LANES]
        q_pair = st_ref[:, q0 + rope_w:q0 + rope_w + LANES]
        ss = jnp.sum(q_n * q_n + 0.5 * (q_pair * q_pair), axis=-1, keepdims=True)
        rinv = lax.rsqrt(ss + ss_eps)
        q_ref[hd, :, 0:QK_NOPE] = (q_n * qg_n * rinv).astype(BF16)
        q_ref[hd, :, QK_NOPE:] = (rope(q_pair, q_gc, q_gs) * rinv).astype(BF16)

        k_n = st_ref[:, ST_K + hd * QK_NOPE:ST_K + (hd + 1) * QK_NOPE]
        rinv = lax.rsqrt(jnp.sum(k_n * k_n, axis=-1, keepdims=True) + kr_ss)
        k_ref[hd, :, 0:QK_NOPE] = (k_n * kg_n * rinv).astype(BF16)
        k_ref[hd, :, QK_NOPE:] = (kr_rot * rinv).astype(BF16)
        v_ref[hd] = sv_ref[:, hd * V_DIM:(hd + 1) * V_DIM]


def _odd_pre_kernel(x_ref, pos_ref, invf_ref, g_ref, w_in_ref, pool_w_ref, pool_scale_ref,
                    qa_g_ref, qb_ref, kva_g_ref, kvb_ref, qg_n_ref, qg_r_ref, qg_sw_ref,
                    kg_n_ref, kg_r_ref, kg_sw_ref,
                    c_ref, q_ref, k_ref, v_ref, st_ref, sv_ref,
                    zbuf_ref, s2buf_ref, s4buf_ref, s8buf_ref, *, tm, seq_tiles):
    t = pl.program_id(0)
    seq_tile = lax.rem(jnp.maximum(t - 1, 0), seq_tiles)
    bufs = (zbuf_ref, s2buf_ref, s4buf_ref, s8buf_ref)

    @pl.when(t == 0)
    def _():
        st_ref[1] = jnp.zeros(st_ref.shape[1:], F32)
        sv_ref[1] = jnp.zeros(sv_ref.shape[1:], BF16)
        for buf in bufs:
            buf[0:POOL_CARRY_ROWS, :] = jnp.zeros((POOL_CARRY_ROWS, buf.shape[1]), F32)

    def step(slot):
        _odd_pre_matmuls(x_ref, g_ref, w_in_ref, qa_g_ref, qb_ref, kva_g_ref, kvb_ref,
                         st_ref.at[slot], sv_ref.at[slot])
        _odd_pre_tail(st_ref.at[1 - slot], sv_ref.at[1 - slot], pos_ref, invf_ref, pool_w_ref,
                      pool_scale_ref,
                      qg_n_ref, qg_r_ref, qg_sw_ref, kg_n_ref, kg_r_ref, kg_sw_ref,
                      c_ref, q_ref, k_ref, v_ref, bufs, seq_tile, tm)

    for slot in range(2):
        pl.when(lax.rem(t, 2) == slot)(functools.partial(step, slot))


def _odd_pre(x, positions, g, w_in, pool_w, pool_scale, qa_g, qb, kva_g, kvb,
             qg_n, qg_r, qg_sw, kg_n, kg_r, kg_sw, *, tm):
    b, s, d = x.shape
    pool_width = pool_w.shape[0]
    qk_pad = QK_NOPE + LANES
    groups = LANES // (QK_ROPE // 2)
    seq_tiles = s // tm
    n_tiles = b * seq_tiles
    pos = positions.reshape(b, seq_tiles, groups, tm // groups).transpose(0, 1, 3, 2)
    inv_freq = ROPE_THETA ** (-jnp.arange(0, QK_ROPE, 2, dtype=F32) / QK_ROPE)
    invf = jnp.tile(inv_freq, groups).reshape(1, LANES)

    lead, trail = _skewed_tile_maps(n_tiles, seq_tiles)

    def head_map(t):
        i, j = trail(t)
        return i, 0, j, 0

    consts = [invf, g, w_in, pool_w, pool_scale, qa_g, qb, kva_g, kvb,
              qg_n, qg_r, qg_sw, kg_n, kg_r, kg_sw]
    head_spec = lambda w: pl.BlockSpec((None, MLA_HEADS, tm, w), head_map)
    carry = lambda w: pltpu.VMEM((tm + POOL_CARRY_ROWS, w), F32)
    return pl.pallas_call(
        functools.partial(_odd_pre_kernel, tm=tm, seq_tiles=seq_tiles),
        grid=(n_tiles + 1,),
        in_specs=[pl.BlockSpec((None, tm, d), lambda t: (*lead(t), 0)),
                  pl.BlockSpec((None, None, tm // groups, groups), lambda t: (*trail(t), 0, 0))]
                 + [_full_spec(c) for c in consts],
        out_specs=[pl.BlockSpec((None, tm, pool_width), lambda t: (*trail(t), 0)),
                   head_spec(qk_pad), head_spec(qk_pad), head_spec(V_DIM)],
        out_shape=[jax.ShapeDtypeStruct((b, s, pool_width), BF16),
                   jax.ShapeDtypeStruct((b, MLA_HEADS, s, qk_pad), BF16),
                   jax.ShapeDtypeStruct((b, MLA_HEADS, s, qk_pad), BF16),
                   jax.ShapeDtypeStruct((b, MLA_HEADS, s, V_DIM), BF16)],
        scratch_shapes=[pltpu.VMEM((2, tm, ST_WIDTH), F32),
                        pltpu.VMEM((2, tm, MLA_HEADS * V_DIM), BF16),
                        carry(pool_width), carry(pool_width), carry(LANES), carry(LANES)],
        compiler_params=_params(1),
        name="odd_pre",
    )(x, pos, *consts)


def _attn_kernel(q_ref, k_ref, v_ref, o_ref, *, tq, tk, hp):
    s_len = q_ref.shape[1]
    masks = {}

    def causal_mask(nk, offset):
        if (nk, offset) not in masks:
            row = lax.broadcasted_iota(jnp.int32, (tq, nk), 0)
            col = lax.broadcasted_iota(jnp.int32, (tq, nk), 1)
            masks[(nk, offset)] = row + offset >= col
        return masks[(nk, offset)]

    def tile(hd, q0, k0, nk, state):
        q = q_ref[hd, q0:q0 + tq, :]
        k = k_ref[hd, k0:k0 + nk, :]
        v = v_ref[hd, k0:k0 + nk, :]
        sc = lax.dot_general(q, k, (((1,), (1,)), ((), ())), preferred_element_type=F32)
        if k0 + nk - 1 > q0:
            sc = jnp.where(causal_mask(nk, q0 - k0), sc, MASK_VALUE)
        cols = [sc[:, c * LANES:(c + 1) * LANES] for c in range(nk // LANES)]
        rmax = jnp.max(functools.reduce(jnp.maximum, cols), axis=-1, keepdims=True)
        if state is None:
            m_new = jnp.broadcast_to(rmax, (tq, LANES))
            ps = [jnp.exp2(c - m_new) for c in cols]
            return m_new, functools.reduce(jnp.add, ps), _dot(_bf16_cat(ps), v)
        m_prev, l_prev, acc_prev = state
        m_new = jnp.maximum(m_prev, rmax)
        alpha = jnp.exp2(m_prev - m_new)
        ps = [jnp.exp2(c - m_new) for c in cols]
        return (m_new, alpha * l_prev + functools.reduce(jnp.add, ps),
                alpha * acc_prev + _dot(_bf16_cat(ps), v))

    for qi in range(s_len // tq):
        for hd in range(hp):
            q0 = qi * tq
            state = None
            for k0 in range(0, q0 + tq, tk):
                state = tile(hd, q0, k0, min(tk, q0 + tq - k0), state)
            _, l_part, acc = state
            l = jnp.sum(l_part, axis=-1, keepdims=True)
            o_ref[qi * tq:(qi + 1) * tq, hd * V_DIM:(hd + 1) * V_DIM] = (acc / l).astype(BF16)


def _bf16_cat(cols):
    return jnp.concatenate(cols, axis=1).astype(BF16)


def _attention(q, k, v, *, tq, tk, hp):
    b, nh, s, qk_pad = q.shape
    qk_spec = pl.BlockSpec((None, hp, s, qk_pad), lambda i, j: (i, j, 0, 0))
    return pl.pallas_call(
        functools.partial(_attn_kernel, tq=tq, tk=tk, hp=hp),
        grid=(b, nh // hp),
        in_specs=[qk_spec, qk_spec,
                  pl.BlockSpec((None, hp, s, V_DIM), lambda i, j: (i, j, 0, 0))],
        out_specs=pl.BlockSpec((None, s, hp * V_DIM), lambda i, j: (i, 0, j)),
        out_shape=jax.ShapeDtypeStruct((b, s, nh * V_DIM), BF16),
        compiler_params=_params(),
        name="mla_attention",
    )(q, k, v)


def _pad_lanes(a, width):
    return jnp.pad(a, [(0, 0)] * (a.ndim - 1) + [(0, width - a.shape[-1])])


def _swap_halves(a, *, negate_first):
    a1, a2 = jnp.split(a, 2, axis=-1)
    return jnp.concatenate([-a2 if negate_first else a2, a1], axis=-1)


def _rope_pair(w):
    return jnp.concatenate([w, _swap_halves(w, negate_first=True)], axis=-1)


class _TileConfig(NamedTuple):
    tm_even: int
    tm_odd: int
    tm_ffn: int
    ff_chunk: int
    tq: int
    tk: int
    heads_per_step: int


def _tile_config(seq_len):
    cfg = _TileConfig(tm_even=1024, tm_odd=512, tm_ffn=1024, ff_chunk=1 * MXU_WIDTH,
                      tq=256, tk=256, heads_per_step=3)
    assert all(seq_len % t == 0 for t in (cfg.tm_even, cfg.tm_odd, cfg.tm_ffn, cfg.tq))
    assert cfg.tk % cfg.tq == 0
    assert MLA_HEADS % cfg.heads_per_step == 0 and cfg.tm_even % SG_CHUNK == 0
    return cfg


def kernel(x, positions, mix_norm, ffn_norm, even_w_in, sg_ln_g, sg_w_s, sg_b_s, sc_conv_w,
           even_w_out, odd_w_in, pool_w, pool_scale, q_a_norm, q_b, kv_a_norm, kv_b, q_norm,
           k_norm, odd_w_out, ffn_w_gate, ffn_w_up, ffn_w_down):
    depth = mix_norm.shape[0]
    cfg = _tile_config(x.shape[1])
    row = lambda a: a.reshape(1, -1)

    for layer in range(depth):
        i = layer // 2
        if layer % 2 == 0:
            mix = _even_mixer(
                x, row(mix_norm[layer]), even_w_in[i].astype(BF16), row(sg_ln_g[i]),
                sg_w_s[i], sg_b_s[i].T, sc_conv_w[i], tm=cfg.tm_even)
            mixes, w_out = [mix], even_w_out
        else:
            pool_width = pool_scale.shape[-1]
            n_groups = pool_w.shape[1]
            gd = pool_w.shape[2]
            w_bd = jnp.zeros((pool_width, pool_width), F32)
            for gidx in range(n_groups):
                w_bd = w_bd.at[gidx * gd:(gidx + 1) * gd, gidx * gd:(gidx + 1) * gd].set(pool_w[i, gidx])
            q_lora = q_b.shape[1]
            qb3 = q_b[i].reshape(q_lora, MLA_HEADS, QK_DIM)
            qb_nope = qb3[:, :, :QK_NOPE].reshape(q_lora, MLA_HEADS * QK_NOPE)
            qb_pair = _rope_pair(qb3[:, :, QK_NOPE:]).reshape(q_lora, MLA_HEADS * LANES)
            qb = jnp.concatenate([qb_nope, qb_pair], axis=1).astype(BF16)
            w_in = jnp.concatenate(
                [odd_w_in[i][:, :-QK_ROPE], _rope_pair(odd_w_in[i][:, -QK_ROPE:])],
                axis=1).astype(BF16)
            gain_rows = []
            for gn in (q_norm[i], k_norm[i]):
                g_rope = row(gn[QK_NOPE:])
                gain_rows += [row(gn[:QK_NOPE]), _pad_lanes(g_rope, LANES),
                              _pad_lanes(_swap_halves(g_rope, negate_first=False), LANES)]
            c_out, q, k, v = _odd_pre(
                x, positions, row(mix_norm[layer]), w_in, w_bd.astype(BF16), row(pool_scale[i]),
                row(q_a_norm[i]), qb, row(kv_a_norm[i]), kv_b[i].astype(BF16), *gain_rows,
                tm=cfg.tm_odd)
            d_out = _attention(q, k, v, tq=cfg.tq, tk=cfg.tk, hp=cfg.heads_per_step)
            mixes, w_out = [c_out, d_out], odd_w_out
        x = _post(x, mixes, w_out, i, row(ffn_norm[layer]), ffn_w_gate, ffn_w_up, ffn_w_down,
                  layer, tm=cfg.tm_ffn, ff_chunk=cfg.ff_chunk)
    return x
```
